```python
import jax
import jax.numpy as jnp
from jax import lax
import numpy as np

D_MODEL = 2048
BATCH = 4
SEQ = 4096
DEPTH = 4

CTX_LEN = 256
GRID_W = 64
D_MIX = D_MODEL
D_RET = D_MIX // 2
D_CONV = D_MIX - D_RET
RET_HEADS = 8
RET_HEAD_DIM = D_RET // RET_HEADS
RET_DECAY_BASE = -5.0
CHUNK = 128
CONV_GROUPS = 8
CONV_WIDTH = 3
D_IN = 4 * D_RET + 3 * D_CONV
D_FF = 5632
N_EXPERTS = 8
TOP_K = 2
D_FF_EXPERT = 4096
ROPE_BASE = 10000.0
EPS = 1e-6
N_DENSE = (DEPTH + 1) // 2
N_MOE = DEPTH // 2

kernel_name = 'hybrid_retention_shortconv_moe_dit'


def rms_norm(t, gain):
    tf = t.astype(jnp.float32)
    tf = tf * lax.rsqrt(jnp.mean(tf * tf, axis=-1, keepdims=True) + EPS)
    return (tf * gain.astype(jnp.float32)).astype(t.dtype)


def modulate(t, shift, scale):
    return t * (1.0 + scale) + shift


def split_heads(t):
    b, l, _ = t.shape
    return t.reshape(b, l, RET_HEADS, RET_HEAD_DIM).transpose(0, 2, 1, 3)


def rope_2d_tables(row, col):
    n_freq = RET_HEAD_DIM // 4
    freqs = ROPE_BASE ** (-jnp.arange(n_freq, dtype=jnp.float32) / n_freq)
    ang_r = row[:, None] * freqs[None, :]
    ang_c = col[:, None] * freqs[None, :]
    return (jnp.cos(ang_r), jnp.sin(ang_r), jnp.cos(ang_c), jnp.sin(ang_c))


def _rotate_half(t, cos, sin):
    t1, t2 = jnp.split(t, 2, axis=-1)
    return jnp.concatenate([t1 * cos - t2 * sin, t1 * sin + t2 * cos], axis=-1)


def apply_rope_2d(t, rope):
    cos_r, sin_r, cos_c, sin_c = rope
    t_row, t_col = jnp.split(t, 2, axis=-1)
    out = jnp.concatenate([_rotate_half(t_row, cos_r, sin_r), _rotate_half(t_col, cos_c, sin_c)], axis=-1)
    return out.astype(t.dtype)


def retention_final_state(k, v, log_gamma):
    seq_len = k.shape[2]
    dist = (seq_len - 1 - jnp.arange(seq_len)).astype(jnp.float32)
    w = jnp.exp(log_gamma[:, None] * dist[None, :])
    return jnp.einsum('bhld,bhle->bhde', k * w[None, :, :, None], v)


def retention_chunked(q, k, v, log_gamma, state0, include_diag):
    b, h, seq_len, dk = q.shape
    dv = v.shape[-1]
    nc = seq_len // CHUNK
    qc = q.reshape(b, h, nc, CHUNK, dk)
    kc = k.reshape(b, h, nc, CHUNK, dk)
    vc = v.reshape(b, h, nc, CHUNK, dv)
    pos = jnp.arange(CHUNK, dtype=jnp.float32)
    diff = pos[:, None] - pos[None, :]
    valid = (diff >= 0) if include_diag else (diff > 0)
    lg = log_gamma[:, None, None]
    decay = jnp.where(valid, jnp.exp(lg * jnp.where(valid, diff, 0.0)), 0.0)
    scores = jnp.einsum('bhncd,bhnmd->bhncm', qc, kc) * decay[None, :, None]
    y_intra = jnp.einsum('bhncm,bhnme->bhnce', scores, vc)
    k_w = jnp.exp(log_gamma[:, None] * (CHUNK - 1.0 - pos)[None, :])
    summ = jnp.einsum('bhncd,bhnce->nbhde', kc * k_w[None, :, None, :, None], vc)
    chunk_decay = jnp.exp(log_gamma * CHUNK)[None, :, None, None]

    def step(state, s):
        return state * chunk_decay + s, state

    _, prev = lax.scan(step, state0, summ)
    q_w = jnp.exp(log_gamma[:, None] * (pos + 1.0)[None, :])
    y_cross = jnp.einsum('bhncd,nbhde->bhnce', qc * q_w[None, :, None, :, None], prev)
    return (y_intra + y_cross).reshape(b, h, seq_len, dv)


def bidir_retention(q, k, v, log_gamma, state_f, state_b):
    flip = lambda t: jnp.flip(t, axis=2)
    y_f = retention_chunked(q, k, v, log_gamma[0], state_f, True)
    y_b = flip(retention_chunked(flip(q), flip(k), flip(v), log_gamma[1], state_b, False))
    return y_f + y_b


def retention_head_out(y, g):
    yf = y.astype(jnp.float32)
    mu = jnp.mean(yf, axis=-1, keepdims=True)
    var = jnp.mean(jnp.square(yf - mu), axis=-1, keepdims=True)
    yn = (yf - mu) * lax.rsqrt(var + EPS)
    b, h, l, dv = y.shape
    yn = yn.transpose(0, 2, 1, 3).reshape(b, l, h * dv).astype(g.dtype)
    return jax.nn.silu(g) * yn


def short_conv_group(b_gate, c_gate, u, conv_w):
    z = c_gate * u
    seq_len = z.shape[1]
    pad = CONV_WIDTH // 2
    zp = jnp.pad(z, ((0, 0), (pad, pad), (0, 0)))
    conv = sum(conv_w[j] * zp[:, j:j + seq_len] for j in range(CONV_WIDTH))
    y = b_gate * conv
    bsz = y.shape[0]
    yg = y.reshape(bsz, seq_len, CONV_GROUPS, D_CONV // CONV_GROUPS).astype(jnp.float32)
    yg = yg * lax.rsqrt(jnp.mean(yg * yg, axis=-1, keepdims=True) + EPS)
    return yg.reshape(bsz, seq_len, D_CONV).astype(y.dtype)


def hybrid_mixer(hx, hc, w_in, w_out, conv_w, decay_exp, rope, need_ctx_out):
    cuts = [D_RET, 2 * D_RET, 3 * D_RET, 4 * D_RET, 4 * D_RET + D_CONV, 4 * D_RET + 2 * D_CONV]
    q_x, k_x, v_x, g_x, b_x, c_x, u_x = jnp.split(hx @ w_in, cuts, axis=-1)
    if need_ctx_out:
        q_c, k_c, v_c, g_c, b_c, c_c, u_c = jnp.split(hc @ w_in, cuts, axis=-1)
    else:
        k_c, v_c = jnp.split(hc @ w_in[:, D_RET:3 * D_RET], 2, axis=-1)
    log_gamma = jnp.log1p(-jnp.exp2(decay_exp.astype(jnp.float32)))
    k_scale = RET_HEAD_DIM ** -0.5
    qx = apply_rope_2d(split_heads(q_x), rope)
    kx = apply_rope_2d(split_heads(k_x), rope) * k_scale
    vx = split_heads(v_x)
    kc = split_heads(k_c) * k_scale
    vc = split_heads(v_c)
    s_f = retention_final_state(kc, vc, log_gamma[0])
    s_b = retention_final_state(jnp.flip(kc, axis=2), jnp.flip(vc, axis=2), log_gamma[1])
    ret_x = bidir_retention(qx, kx, vx, log_gamma, s_f, s_b)
    mix_x = jnp.concatenate([retention_head_out(ret_x, g_x), short_conv_group(b_x, c_x, u_x, conv_w)], axis=-1)
    out_x = mix_x @ w_out
    if not need_ctx_out:
        return out_x, None
    qc = split_heads(q_c)
    zero = jnp.zeros_like(s_f)
    ret_c = bidir_retention(qc, kc, vc, log_gamma, zero, zero)
    mix_c = jnp.concatenate([retention_head_out(ret_c, g_c), short_conv_group(b_c, c_c, u_c, conv_w)], axis=-1)
    return out_x, mix_c @ w_out


def swiglu(h, w1, w3, w2):
    return (jax.nn.silu(h @ w1) * (h @ w3)) @ w2


def moe_swiglu(h, router, w1, w3, w2):
    logits = (h @ router).astype(jnp.float32)
    probs = jax.nn.softmax(logits, axis=-1)
    top_p, top_i = lax.top_k(probs, TOP_K)
    top_p = top_p / jnp.sum(top_p, axis=-1, keepdims=True)
    out = jnp.zeros_like(h)
    for e in range(N_EXPERTS):
        gate = jnp.sum(jnp.where(top_i == e, top_p, 0.0), axis=-1, keepdims=True).astype(h.dtype)
        out = out + gate * swiglu(h, w1[e], w3[e], w2[e])
    return out


def setup_inputs(seed: int = 0) -> dict:
    key = jax.random.key(seed)
    ks = jax.random.split(key, 18)
    f32 = jnp.float32

    def nrm(k, shape, scale):
        return jax.random.normal(k, shape, f32) * scale

    x = nrm(ks[0], (BATCH, SEQ, D_MODEL), 1.0)
    c = nrm(ks[1], (BATCH, D_MODEL), 1.0)
    ctx = nrm(ks[2], (BATCH, CTX_LEN, D_MODEL), 1.0)
    c_ctx = nrm(ks[3], (D_MODEL,), 1.0)
    w_in = nrm(ks[4], (DEPTH, D_MODEL, D_IN), D_MODEL ** -0.5)
    w_out = nrm(ks[5], (DEPTH, D_MIX, D_MODEL), D_MIX ** -0.5)
    conv_w = nrm(ks[6], (DEPTH, CONV_WIDTH, D_CONV), CONV_WIDTH ** -0.5)
    ret_decay = (RET_DECAY_BASE - jnp.arange(RET_HEADS, dtype=f32))[None, None, :] + nrm(ks[7], (DEPTH, 2, RET_HEADS), 0.1)
    ada_w = nrm(ks[8], (DEPTH, D_MODEL, 6 * D_MODEL), 0.5 * D_MODEL ** -0.5)
    ada_b = nrm(ks[9], (DEPTH, 6 * D_MODEL), 0.01)
    norm_w = 1.0 + nrm(ks[10], (DEPTH, 4, D_MODEL), 0.02)
    ffn_w1 = nrm(ks[11], (N_DENSE, D_MODEL, D_FF), D_MODEL ** -0.5)
    ffn_w3 = nrm(ks[12], (N_DENSE, D_MODEL, D_FF), D_MODEL ** -0.5)
    ffn_w2 = nrm(ks[13], (N_DENSE, D_FF, D_MODEL), D_FF ** -0.5)
    router = nrm(ks[14], (N_MOE, D_MODEL, N_EXPERTS), D_MODEL ** -0.5)
    moe_w1 = nrm(ks[15], (N_MOE, N_EXPERTS, D_MODEL, D_FF_EXPERT), D_MODEL ** -0.5)
    moe_w3 = nrm(ks[16], (N_MOE, N_EXPERTS, D_MODEL, D_FF_EXPERT), D_MODEL ** -0.5)
    moe_w2 = nrm(ks[17], (N_MOE, N_EXPERTS, D_FF_EXPERT, D_MODEL), D_FF_EXPERT ** -0.5)
    return {'x': x, 'c': c, 'ctx': ctx, 'c_ctx': c_ctx, 'w_in': w_in, 'w_out': w_out,
            'conv_w': conv_w, 'ret_decay': ret_decay, 'ada_w': ada_w, 'ada_b': ada_b,
            'norm_w': norm_w, 'ffn_w1': ffn_w1, 'ffn_w3': ffn_w3, 'ffn_w2': ffn_w2,
            'router': router, 'moe_w1': moe_w1, 'moe_w3': moe_w3, 'moe_w2': moe_w2}


def reference(x, c, ctx, c_ctx, w_in, w_out, conv_w, ret_decay, ada_w, ada_b, norm_w,
              ffn_w1, ffn_w3, ffn_w2, router, moe_w1, moe_w3, moe_w2):
    n_rows = x.shape[1] // GRID_W
    row = jnp.repeat(jnp.arange(n_rows, dtype=jnp.float32), GRID_W)
    col = jnp.tile(jnp.arange(GRID_W, dtype=jnp.float32), n_rows)
    rope = rope_2d_tables(row, col)
    n_ctx = ctx.shape[1]
    for layer in range(DEPTH):
        need_ctx_out = layer < DEPTH - 1
        mod_x = jnp.split((jax.nn.silu(c) @ ada_w[layer] + ada_b[layer])[:, None, :], 6, axis=-1)
        mod_c = jnp.split((jax.nn.silu(c_ctx) @ ada_w[layer] + ada_b[layer])[None, None, :], 6, axis=-1)
        hx = modulate(rms_norm(x, norm_w[layer, 0]), mod_x[0], mod_x[1])
        hc = modulate(rms_norm(ctx, norm_w[layer, 0]), mod_c[0], mod_c[1])
        out_x, out_c = hybrid_mixer(hx, hc, w_in[layer], w_out[layer], conv_w[layer],
                                    ret_decay[layer], rope, need_ctx_out)
        x = x + mod_x[2] * rms_norm(out_x, norm_w[layer, 1])
        if need_ctx_out:
            ctx = ctx + mod_c[2] * rms_norm(out_c, norm_w[layer, 1])
        hx = modulate(rms_norm(x, norm_w[layer, 2]), mod_x[3], mod_x[4])
        if need_ctx_out:
            hc = modulate(rms_norm(ctx, norm_w[layer, 2]), mod_c[3], mod_c[4])
            h = jnp.concatenate([hc, hx], axis=1)
        else:
            h = hx
        i = layer // 2
        if layer % 2 == 0:
            y = swiglu(h, ffn_w1[i], ffn_w3[i], ffn_w2[i])
        else:
            y = moe_swiglu(h, router[i], moe_w1[i], moe_w3[i], moe_w2[i])
        y = rms_norm(y, norm_w[layer, 3])
        if need_ctx_out:
            ctx = ctx + mod_c[5] * y[:, :n_ctx]
            x = x + mod_x[5] * y[:, n_ctx:]
        else:
            x = x + mod_x[5] * y
    return x
```

```python
import functools

import jax
import jax.numpy as jnp
from jax import lax
from jax.experimental import pallas as pl
from jax.experimental.pallas import tpu as pltpu

RET_HEADS = 8
HEAD_DIM = 128
CHUNK = 128
CONV_GROUPS = 8
CONV_WIDTH = 3
GRID_W = 64
ROPE_BASE = 10000.0
EPS = 1e-6
N_EXPERTS = 8
LANES = 128
MOD_ROWS = 8
VMEM_LIMIT = 56 * 1024 * 1024

F32 = jnp.float32
BF16 = jnp.bfloat16


def _params(semantics):
    return pltpu.CompilerParams(dimension_semantics=semantics, vmem_limit_bytes=VMEM_LIMIT)


def _pick_tile(target, *sizes):
    t = target
    while t > 8 and any(s % t for s in sizes):
        t //= 2
    assert all(s % t == 0 for s in sizes), (target, sizes)
    return t


def _silu(v):
    return v * jax.nn.sigmoid(v)


def _rms(v, gain):
    return v * lax.rsqrt(jnp.mean(v * v, axis=-1, keepdims=True) + EPS) * gain


def _ada_kernel(c_ref, w_ref, b_ref, o_ref):
    a = _silu(c_ref[...])
    o_ref[...] = jnp.dot(a, w_ref[...], preferred_element_type=F32,
                         precision=lax.Precision.HIGHEST) + b_ref[...]


def _ada_table(c8, ada_w, ada_b):
    depth, d, n = ada_w.shape
    tn = _pick_tile(1024, n)
    return pl.pallas_call(
        _ada_kernel,
        grid=(depth, n // tn),
        in_specs=[pl.BlockSpec((MOD_ROWS, d), lambda l, j: (0, 0)),
                  pl.BlockSpec((None, d, tn), lambda l, j: (l, 0, j)),
                  pl.BlockSpec((None, 1, tn), lambda l, j: (l, 0, j))],
        out_specs=pl.BlockSpec((None, MOD_ROWS, tn), lambda l, j: (l, 0, j)),
        out_shape=jax.ShapeDtypeStruct((depth, MOD_ROWS, n), F32),
        compiler_params=_params(("arbitrary", "arbitrary")),
        name="ada_table",
    )(c8, ada_w, ada_b.reshape(depth, 1, n))


class _Rows:
    def __init__(self, tm, d, seq, batch, layer, extra_grid=0):
        self.tm, self.d, self.layer = tm, d, layer
        self.tiles_per_batch = seq // tm
        self.batch = batch
        self.extra = extra_grid

    def _wrap(self, fn):
        if self.extra == 0:
            return lambda t: fn(t)
        return lambda t, f, *_: fn(t)

    def tile(self, width=None):
        return pl.BlockSpec((self.tm, width or self.d), self._wrap(lambda t: (t, 0)))

    def mod(self, which, layer=None):
        layer = self.layer if layer is None else layer
        tpb, b = self.tiles_per_batch, self.batch
        return pl.BlockSpec(
            (None, 1, self.d),
            self._wrap(lambda t: (layer * MOD_ROWS + jnp.minimum(t // tpb, b), 0, which)))

    def gain(self, which, layer=None):
        layer = self.layer if layer is None else layer
        return pl.BlockSpec((None, 1, self.d), self._wrap(lambda t: (layer * 4 + which, 0, 0)))


def _prologue_kernel(x_ref, g_ref, sh_ref, sc_ref, h_ref):
    h_ref[...] = (_rms(x_ref[...], g_ref[...]) * (1.0 + sc_ref[...]) + sh_ref[...]).astype(h_ref.dtype)


def _prologue(xs, mod, gains, seq, batch, tm):
    t_rows, d = xs.shape
    rows = _Rows(tm, d, seq, batch, 0)
    return pl.pallas_call(
        _prologue_kernel,
        grid=(t_rows // tm,),
        in_specs=[rows.tile(), rows.gain(0), rows.mod(0), rows.mod(1)],
        out_specs=rows.tile(),
        out_shape=jax.ShapeDtypeStruct((t_rows, d), BF16),
        compiler_params=_params(("arbitrary",)),
        name="prologue_norm",
    )(xs, gains, mod, mod)


def _matmul_kernel(h_ref, w_ref, o_ref):
    o_ref[...] = jnp.dot(h_ref[...], w_ref[...], preferred_element_type=F32).astype(o_ref.dtype)


def _in_proj(h, w, tm):
    t_rows, d = h.shape
    n = w.shape[1]
    tn = _pick_tile(1024, n)
    return pl.pallas_call(
        _matmul_kernel,
        grid=(n // tn, t_rows // tm),
        in_specs=[pl.BlockSpec((tm, d), lambda j, i: (i, 0)),
                  pl.BlockSpec((d, tn), lambda j, i: (0, j))],
        out_specs=pl.BlockSpec((tm, tn), lambda j, i: (i, j)),
        out_shape=jax.ShapeDtypeStruct((t_rows, n), BF16),
        compiler_params=_params(("arbitrary", "arbitrary")),
        name="in_proj",
    )(h, w)


def _swap32(t, lane_lo):
    return jnp.where(lane_lo, pltpu.roll(t, 96, 1), pltpu.roll(t, 32, 1))


def _mixer_kernel(*refs, seq_len, rope, state_in, state_out):
    it = iter(refs)
    q_ref, k_ref, v_ref, g_ref, b_ref, c_ref, u_ref = (next(it) for _ in range(7))
    dec_ref, cw_ref = next(it), next(it)
    cos_ref = sin_ref = st_ref = None
    if rope:
        cos_ref, sin_ref = next(it), next(it)
    if state_in:
        st_ref = next(it)
    ret_ref, conv_ref = next(it), next(it)
    sto_ref = next(it) if state_out else None
    qs, ks, sb, zpad = next(it), next(it), next(it), next(it)

    n_chunks = seq_len // CHUNK
    blk = min(256, seq_len)

    lg = jnp.log1p(-jnp.exp2(dec_ref[...]))
    lgf = jnp.broadcast_to(lg[0, 0:1, :], (CHUNK, CHUNK))
    lgb = jnp.broadcast_to(lg[1, 0:1, :], (CHUNK, CHUNK))
    ii = lax.broadcasted_iota(jnp.int32, (CHUNK, CHUNK), 0).astype(F32)
    jj = lax.broadcasted_iota(jnp.int32, (CHUNK, CHUNK), 1).astype(F32)
    diff = ii - jj
    decay = jnp.where(diff >= 0, jnp.exp(lgf * jnp.maximum(diff, 0.0)),
                      jnp.exp(lgb * jnp.maximum(-diff, 0.0)))
    wq_f = jnp.exp(lgf * (ii + 1.0))
    wq_b = jnp.exp(lgb * (CHUNK - ii))
    wk_f = jnp.exp(lgf * (CHUNK - 1.0 - ii))
    wk_b = jnp.exp(lgb * ii)
    gc_f = jnp.exp(lgf * CHUNK)
    gc_b = jnp.exp(lgb * CHUNK)
    k_scale = HEAD_DIM ** -0.5

    lane = lax.broadcasted_iota(jnp.int32, (blk, LANES), 1)
    lane_lo = (lane % 64) < 32
    for r0 in range(0, seq_len, blk):
        rows = pl.ds(r0, blk)
        q = q_ref[rows, :].astype(F32)
        k = k_ref[rows, :].astype(F32)
        if rope:
            cs, sn = cos_ref[rows, :], sin_ref[rows, :]
            q = q * cs + _swap32(q, lane_lo) * sn
            k = k * cs + _swap32(k, lane_lo) * sn
        qs[rows, :] = q.astype(BF16)
        ks[rows, :] = (k * k_scale).astype(BF16)

    def chunk(n):
        return pl.ds(pl.multiple_of(n * CHUNK, CHUNK), CHUNK)

    def summary(n, wk):
        kw = (ks[chunk(n), :].astype(F32) * wk).T.astype(BF16)
        return jnp.dot(kw, v_ref[chunk(n), :], preferred_element_type=F32)

    def bwd_step(i, state):
        n = n_chunks - 1 - i
        sb[n] = state
        return state * gc_b + summary(n, wk_b)

    s_b0 = st_ref[1] if state_in else jnp.zeros((HEAD_DIM, HEAD_DIM), F32)
    s_b_final = lax.fori_loop(0, n_chunks, bwd_step, s_b0)

    def fwd_step(n, state):
        qn = qs[chunk(n), :]
        kn = ks[chunk(n), :]
        vn = v_ref[chunk(n), :]
        scores = lax.dot_general(qn, kn, (((1,), (1,)), ((), ())), preferred_element_type=F32)
        y = jnp.dot((scores * decay).astype(BF16), vn, preferred_element_type=F32)
        qf = qn.astype(F32)
        y = y + jnp.dot((qf * wq_f).astype(BF16), state.astype(BF16), preferred_element_type=F32)
        y = y + jnp.dot((qf * wq_b).astype(BF16), sb[n].astype(BF16), preferred_element_type=F32)
        mu = jnp.mean(y, axis=-1, keepdims=True)
        yc = y - mu
        yn = yc * lax.rsqrt(jnp.mean(yc * yc, axis=-1, keepdims=True) + EPS)
        gate = g_ref[chunk(n), :].astype(F32)
        ret_ref[chunk(n), :] = (_silu(gate) * yn).astype(ret_ref.dtype)
        return state * gc_f + summary(n, wk_f)

    s_f0 = st_ref[0] if state_in else jnp.zeros((HEAD_DIM, HEAD_DIM), F32)
    s_f_final = lax.fori_loop(0, n_chunks, fwd_step, s_f0)
    if state_out:
        sto_ref[0] = s_f_final
        sto_ref[1] = s_b_final

    zpad[pl.ds(0, 8), :] = jnp.zeros((8, LANES), F32)
    zpad[pl.ds(8 + seq_len, 8), :] = jnp.zeros((8, LANES), F32)
    for r0 in range(0, seq_len, blk):
        rows = pl.ds(r0, blk)
        zpad[pl.ds(8 + r0, blk), :] = c_ref[rows, :].astype(F32) * u_ref[rows, :].astype(F32)
    cw = cw_ref[...]
    for r0 in range(0, seq_len, blk):
        rows = pl.ds(r0, blk)
        conv = (cw[0:1, :] * zpad[pl.ds(7 + r0, blk), :] + cw[1:2, :] * zpad[pl.ds(8 + r0, blk), :]
                + cw[2:3, :] * zpad[pl.ds(9 + r0, blk), :])
        y = b_ref[rows, :].astype(F32) * conv
        conv_ref[rows, :] = (y * lax.rsqrt(jnp.mean(y * y, axis=-1, keepdims=True) + EPS)
                             ).astype(conv_ref.dtype)


def _mixer(proj, dec, conv_w, rope_tabs, states, *, batch, seq_len, row0):
    d_half = RET_HEADS * HEAD_DIM
    rb0 = row0 // seq_len
    rope = rope_tabs is not None
    state_in = states is not None

    def col(cb):
        return pl.BlockSpec((seq_len, LANES), lambda b, j: (rb0 + b, cb * RET_HEADS + j))

    in_specs = [col(i) for i in range(7)]
    args = [proj] * 7
    in_specs += [pl.BlockSpec((2, None, 8, LANES), lambda b, j: (0, j, 0, 0)),
                 pl.BlockSpec((CONV_WIDTH, LANES), lambda b, j: (0, j))]
    args += [dec, conv_w]
    if rope:
        in_specs += [pl.BlockSpec((seq_len, LANES), lambda b, j: (0, 0))] * 2
        args += list(rope_tabs)
    if state_in:
        in_specs.append(pl.BlockSpec((None, None, 2, HEAD_DIM, HEAD_DIM), lambda b, j: (b, j, 0, 0, 0)))
        args.append(states)
    out_spec = pl.BlockSpec((seq_len, LANES), lambda b, j: (b, j))
    out_specs = [out_spec, out_spec]
    out_shape = [jax.ShapeDtypeStruct((batch * seq_len, d_half), BF16)] * 2
    if not state_in:
        out_specs.append(pl.BlockSpec((None, None, 2, HEAD_DIM, HEAD_DIM), lambda b, j: (b, j, 0, 0, 0)))
        out_shape.append(jax.ShapeDtypeStruct((batch, RET_HEADS, 2, HEAD_DIM, HEAD_DIM), F32))
    return pl.pallas_call(
        functools.partial(_mixer_kernel, seq_len=seq_len, rope=rope, state_in=state_in,
                          state_out=not state_in),
        grid=(batch, RET_HEADS),
        in_specs=in_specs,
        out_specs=out_specs,
        out_shape=out_shape,
        scratch_shapes=[pltpu.VMEM((seq_len, LANES), BF16), pltpu.VMEM((seq_len, LANES), BF16),
                        pltpu.VMEM((seq_len // CHUNK, HEAD_DIM, HEAD_DIM), F32),
                        pltpu.VMEM((seq_len + 16, LANES), F32)],
        compiler_params=_params(("arbitrary", "arbitrary")),
        name="mixer_latent" if state_in else "mixer_context",
    )(*args)


def _residual_next(y, x, gain_post, gate, nxt):
    x_new = x + gate * _rms(y, gain_post)
    if nxt is None:
        return x_new, None
    gain_n, shift_n, scale_n = nxt
    return x_new, _rms(x_new, gain_n) * (1.0 + scale_n) + shift_n


def _out_proj_kernel(lr_ref, lc_ref, cr_ref, cc_ref, w_ref, x_ref, gp_ref, gt_ref, gn_ref, sh_ref,
                     sc_ref, xo_ref, h_ref, *, n_lat_tiles):
    half = lr_ref.shape[1]

    def finish(mr_ref, mc_ref):
        y = jnp.dot(mr_ref[...], w_ref[pl.ds(0, half), :], preferred_element_type=F32)
        y = y + jnp.dot(mc_ref[...], w_ref[pl.ds(half, half), :], preferred_element_type=F32)
        x_new, h = _residual_next(y, x_ref[...], gp_ref[...], gt_ref[...],
                                  (gn_ref[...], sh_ref[...], sc_ref[...]))
        xo_ref[...] = x_new
        h_ref[...] = h.astype(h_ref.dtype)

    @pl.when(pl.program_id(0) < n_lat_tiles)
    def _():
        finish(lr_ref, lc_ref)

    @pl.when(pl.program_id(0) >= n_lat_tiles)
    def _():
        finish(cr_ref, cc_ref)


def _out_proj(mix_lat, mix_ctx, w_out, xs, mod, gains, layer, seq, batch, n_rows, tm, h_dtype):
    d = xs.shape[1]
    half = mix_lat[0].shape[1]
    rows = _Rows(tm, d, seq, batch, layer)
    n_lat_tiles = mix_lat[0].shape[0] // tm
    n_ctx_tiles = mix_ctx[0].shape[0] // tm
    lat_spec = pl.BlockSpec((tm, half), lambda t: (jnp.minimum(t, n_lat_tiles - 1), 0))
    ctx_spec = pl.BlockSpec((tm, half), lambda t: (jnp.clip(t - n_lat_tiles, 0, n_ctx_tiles - 1), 0))
    return pl.pallas_call(
        functools.partial(_out_proj_kernel, n_lat_tiles=n_lat_tiles),
        grid=(n_rows // tm,),
        in_specs=[lat_spec, lat_spec, ctx_spec, ctx_spec, pl.BlockSpec((d, d), lambda t: (0, 0)),
                  rows.tile(), rows.gain(1), rows.mod(2), rows.gain(2), rows.mod(3), rows.mod(4)],
        out_specs=[rows.tile(), rows.tile()],
        out_shape=[jax.ShapeDtypeStruct((n_rows, d), F32), jax.ShapeDtypeStruct((n_rows, d), h_dtype)],
        compiler_params=_params(("arbitrary",)),
        name="out_proj",
    )(mix_lat[0], mix_lat[1], mix_ctx[0], mix_ctx[1], w_out, xs, gains, mod, gains, mod, mod)


def _ffn_kernel(h_ref, w1_ref, w3_ref, w2_ref, x_ref, gp_ref, gt_ref, gn_ref, sh_ref, sc_ref,
                xo_ref, hn_ref, acc_ref):
    f = pl.program_id(1)
    h = h_ref[...]
    a = (_silu(jnp.dot(h, w1_ref[...], preferred_element_type=F32))
         * jnp.dot(h, w3_ref[...], preferred_element_type=F32)).astype(BF16)
    part = jnp.dot(a, w2_ref[...], preferred_element_type=F32)

    @pl.when(f == 0)
    def _():
        acc_ref[...] = part

    @pl.when(f > 0)
    def _():
        acc_ref[...] += part

    @pl.when(f == pl.num_programs(1) - 1)
    def _():
        x_new, hn = _residual_next(acc_ref[...], x_ref[...], gp_ref[...], gt_ref[...],
                                   (gn_ref[...], sh_ref[...], sc_ref[...]))
        xo_ref[...] = x_new
        hn_ref[...] = hn.astype(hn_ref.dtype)


def _dense_ffn(h, w1, w3, w2, xs, mod, gains, layer, seq, batch, tm):
    t_rows, d = h.shape
    ff = w1.shape[1]
    tf = _pick_tile(512, ff)
    rows = _Rows(tm, d, seq, batch, layer, extra_grid=1)
    return pl.pallas_call(
        _ffn_kernel,
        grid=(t_rows // tm, ff // tf),
        in_specs=[rows.tile(),
                  pl.BlockSpec((d, tf), lambda t, f: (0, f)),
                  pl.BlockSpec((d, tf), lambda t, f: (0, f)),
                  pl.BlockSpec((tf, d), lambda t, f: (f, 0)),
                  rows.tile(), rows.gain(3), rows.mod(5),
                  rows.gain(0, layer + 1), rows.mod(0, layer + 1), rows.mod(1, layer + 1)],
        out_specs=[rows.tile(), rows.tile()],
        out_shape=[jax.ShapeDtypeStruct((t_rows, d), F32), jax.ShapeDtypeStruct((t_rows, d), BF16)],
        scratch_shapes=[pltpu.VMEM((tm, d), F32)],
        compiler_params=_params(("arbitrary", "arbitrary")),
        name="dense_ffn",
    )(h, w1, w3, w2, xs, gains, mod, gains, mod, mod)


def _router_kernel(h_ref, r_ref, o_ref):
    logits = jnp.dot(h_ref[...], r_ref[...], preferred_element_type=F32,
                     precision=lax.Precision.HIGHEST)
    lane = lax.broadcasted_iota(jnp.int32, logits.shape, 1)
    neg = jnp.float32(-jnp.inf)
    l1 = jnp.where(lane < N_EXPERTS, logits, neg)
    m1 = jnp.max(l1, axis=-1, keepdims=True)
    i1 = jnp.min(jnp.where(l1 == m1, lane, LANES), axis=-1, keepdims=True)
    l2 = jnp.where(lane == i1, neg, l1)
    m2 = jnp.max(l2, axis=-1, keepdims=True)
    i2 = jnp.min(jnp.where(l2 == m2, lane, LANES), axis=-1, keepdims=True)
    e2 = jnp.exp(m2 - m1)
    p1 = 1.0 / (1.0 + e2)
    p2 = e2 / (1.0 + e2)
    o_ref[...] = jnp.where(lane == 0, i1.astype(F32),
                           jnp.where(lane == 1, i2.astype(F32),
                                     jnp.where(lane == 2, p1, jnp.where(lane == 3, p2, 0.0))))


def _router(h, router_pad, tm):
    t_rows, d = h.shape
    return pl.pallas_call(
        _router_kernel,
        grid=(t_rows // tm,),
        in_specs=[pl.BlockSpec((tm, d), lambda t: (t, 0)), pl.BlockSpec((d, LANES), lambda t: (0, 0))],
        out_specs=pl.BlockSpec((tm, LANES), lambda t: (t, 0)),
        out_shape=jax.ShapeDtypeStruct((t_rows, LANES), F32),
        compiler_params=_params(("arbitrary",)),
        name="router",
    )(h, router_pad)


def _dispatch_plan(rinfo, tm):
    t_rows = rinfo.shape[0]
    n_flat = 2 * t_rows
    n_tiles = n_flat // tm + N_EXPERTS
    e_flat = rinfo[:, 0:2].astype(jnp.int32).T.reshape(-1)
    w_flat = rinfo[:, 2:4].T.reshape(-1)
    order = jnp.argsort(e_flat, stable=True).astype(jnp.int32)
    counts = jnp.sum(e_flat[:, None] == jnp.arange(N_EXPERTS, dtype=jnp.int32)[None, :], axis=0,
                     dtype=jnp.int32)
    padded = (counts + tm - 1) // tm * tm
    ends_p = jnp.cumsum(padded)
    starts_p = ends_p - padded
    starts_u = jnp.cumsum(counts) - counts
    tile_start = jnp.arange(n_tiles, dtype=jnp.int32) * tm
    tile_e = jnp.minimum(jnp.searchsorted(ends_p, tile_start, side="right").astype(jnp.int32),
                         N_EXPERTS - 1)
    tile_nv = jnp.where(tile_start < ends_p[-1],
                        jnp.clip(counts[tile_e] - (tile_start - starts_p[tile_e]), 0, tm), 0)
    r = jnp.arange(n_tiles * tm, dtype=jnp.int32)
    e_r = tile_e[r // tm]
    local = r - starts_p[e_r]
    valid = (local < counts[e_r]) & (r < ends_p[-1])
    flat = order[jnp.clip(local + starts_u[e_r], 0, n_flat - 1)]
    src_tok = jnp.where(valid, flat % t_rows, 0).astype(jnp.int32)
    dst_row = jnp.where(valid, flat, 0).astype(jnp.int32)
    w_row = jnp.where(valid, w_flat[flat], 0.0).astype(F32)[:, None]
    last_used = jnp.maximum(jnp.sum(tile_nv > 0) - 1, 0)
    tile_e = jnp.where(tile_nv > 0, tile_e, tile_e[last_used])
    return tile_e.astype(jnp.int32), tile_nv.astype(jnp.int32), src_tok, dst_row, w_row


def _moe_kernel(te_ref, nv_ref, src_ref, dst_ref, h_hbm, w1_ref, w3_ref, w2_ref, wr_ref,
                y_hbm, hbuf, hb, acc_ref, sem_g, sem_s, *, tm):
    t = pl.program_id(0)
    f = pl.program_id(1)
    n_f = pl.num_programs(1)
    base = t * tm

    @pl.when(nv_ref[t] > 0)
    def _():
        @pl.when(f == 0)
        def _():
            def issue(r, carry):
                tok = src_ref[base + r]
                pltpu.make_async_copy(h_hbm.at[pl.ds(tok, 1), :], hbuf.at[pl.ds(r, 1), :], sem_g).start()
                return carry

            lax.fori_loop(0, tm, issue, 0)
            pltpu.make_async_copy(h_hbm.at[pl.ds(0, tm), :], hbuf, sem_g).wait()
            hb[...] = hbuf[...].astype(BF16)

        h = hb[...]
        a = (_silu(jnp.dot(h, w1_ref[...], preferred_element_type=F32))
             * jnp.dot(h, w3_ref[...], preferred_element_type=F32)).astype(BF16)
        part = jnp.dot(a, w2_ref[...], preferred_element_type=F32)

        @pl.when(f == 0)
        def _():
            acc_ref[...] = part

        @pl.when(f > 0)
        def _():
            acc_ref[...] += part

        @pl.when(f == n_f - 1)
        def _():
            acc_ref[...] = acc_ref[...] * wr_ref[...]

            def issue(r, carry):
                row = dst_ref[base + r]
                pltpu.make_async_copy(acc_ref.at[pl.ds(r, 1), :], y_hbm.at[pl.ds(row, 1), :], sem_s).start()
                return carry

            n_valid = nv_ref[t]
            lax.fori_loop(0, n_valid, issue, 0)

            def drain(r, carry):
                pltpu.make_async_copy(acc_ref.at[pl.ds(0, 1), :], y_hbm.at[pl.ds(0, 1), :], sem_s).wait()
                return carry

            lax.fori_loop(0, n_valid, drain, 0)


def _moe_ffn(h, plan, w1, w3, w2, tm):
    tile_e, tile_nv, src_tok, dst_row, w_row = plan
    t_rows, d = h.shape
    ff = w1.shape[2]
    tf = _pick_tile(512, ff)
    n_tiles = tile_e.shape[0]
    n_f = ff // tf

    def f_eff(t, f, nv):
        return jnp.where(nv[t] > 0, f, n_f - 1)

    grid_spec = pltpu.PrefetchScalarGridSpec(
        num_scalar_prefetch=4,
        grid=(n_tiles, n_f),
        in_specs=[pl.BlockSpec(memory_space=pl.ANY),
                  pl.BlockSpec((None, d, tf), lambda t, f, te, nv, s, dd: (te[t], 0, f_eff(t, f, nv))),
                  pl.BlockSpec((None, d, tf), lambda t, f, te, nv, s, dd: (te[t], 0, f_eff(t, f, nv))),
                  pl.BlockSpec((None, tf, d), lambda t, f, te, nv, s, dd: (te[t], f_eff(t, f, nv), 0)),
                  pl.BlockSpec((tm, 1), lambda t, f, te, nv, s, dd: (t, 0))],
        out_specs=pl.BlockSpec(memory_space=pl.ANY),
        scratch_shapes=[pltpu.VMEM((tm, d), F32), pltpu.VMEM((tm, d), BF16), pltpu.VMEM((tm, d), F32),
                        pltpu.SemaphoreType.DMA(()), pltpu.SemaphoreType.DMA(())])
    return pl.pallas_call(
        functools.partial(_moe_kernel, tm=tm),
        grid_spec=grid_spec,
        out_shape=jax.ShapeDtypeStruct((2 * t_rows, d), F32),
        compiler_params=_params(("arbitrary", "arbitrary")),
        name="moe_ffn",
    )(tile_e, tile_nv, src_tok, dst_row, h, w1, w3, w2, w_row)


def _combine_kernel(*refs, emit_next):
    if emit_next:
        y0_ref, y1_ref, x_ref, gp_ref, gt_ref, gn_ref, sh_ref, sc_ref, xo_ref, hn_ref = refs
        nxt = (gn_ref[...], sh_ref[...], sc_ref[...])
    else:
        y0_ref, y1_ref, x_ref, gp_ref, gt_ref, xo_ref = refs
        nxt = None
    x_new, hn = _residual_next(y0_ref[...] + y1_ref[...], x_ref[...], gp_ref[...], gt_ref[...], nxt)
    xo_ref[...] = x_new
    if emit_next:
        hn_ref[...] = hn.astype(hn_ref.dtype)


def _moe_combine(y2, xs, mod, gains, layer, seq, batch, tm, emit_next):
    t_rows, d = xs.shape
    rows = _Rows(tm, d, seq, batch, layer)
    slot1 = t_rows // tm
    in_specs = [rows.tile(), pl.BlockSpec((tm, d), lambda t: (slot1 + t, 0)), rows.tile(),
                rows.gain(3), rows.mod(5)]
    args = [y2, y2, xs, gains, mod]
    out_specs = [rows.tile()]
    out_shape = [jax.ShapeDtypeStruct((t_rows, d), F32)]
    if emit_next:
        in_specs += [rows.gain(0, layer + 1), rows.mod(0, layer + 1), rows.mod(1, layer + 1)]
        args += [gains, mod, mod]
        out_specs.append(rows.tile())
        out_shape.append(jax.ShapeDtypeStruct((t_rows, d), BF16))
    out = pl.pallas_call(
        functools.partial(_combine_kernel, emit_next=emit_next),
        grid=(t_rows // tm,),
        in_specs=in_specs,
        out_specs=out_specs,
        out_shape=out_shape,
        compiler_params=_params(("arbitrary",)),
        name="moe_combine",
    )(*args)
    return (out[0], out[1]) if emit_next else (out[0], None)


def _rope_tables(seq):
    pos = jnp.arange(seq, dtype=jnp.int32)
    row = (pos // GRID_W).astype(F32)
    col = (pos % GRID_W).astype(F32)
    n_freq = HEAD_DIM // 4
    freqs = ROPE_BASE ** (-jnp.arange(n_freq, dtype=F32) / n_freq)
    ang_r = row[:, None] * freqs[None, :]
    ang_c = col[:, None] * freqs[None, :]
    cos = jnp.concatenate([jnp.cos(ang_r), jnp.cos(ang_r), jnp.cos(ang_c), jnp.cos(ang_c)], axis=-1)
    sin = jnp.concatenate([-jnp.sin(ang_r), jnp.sin(ang_r), -jnp.sin(ang_c), jnp.sin(ang_c)], axis=-1)
    return cos, sin


def kernel(x, c, ctx, c_ctx, w_in, w_out, conv_w, ret_decay, ada_w, ada_b, norm_w, ffn_w1, ffn_w3,
           ffn_w2, router, moe_w1, moe_w3, moe_w2):
    batch, seq, d = x.shape
    n_ctx = ctx.shape[1]
    depth = w_in.shape[0]
    n_lat = batch * seq
    t_all = n_lat + batch * n_ctx
    assert d == 2 * RET_HEADS * HEAD_DIM and batch + 1 <= MOD_ROWS
    assert seq % CHUNK == 0 and n_ctx % CHUNK == 0 and n_lat % n_ctx == 0

    tm_small = _pick_tile(256, seq, batch * n_ctx)
    tm_ffn = _pick_tile(512, seq, batch * n_ctx)
    tm_proj = _pick_tile(1024, seq, batch * n_ctx)

    c8 = jnp.zeros((MOD_ROWS, d), F32).at[:batch].set(c).at[batch].set(c_ctx)
    mod = _ada_table(c8, ada_w, ada_b).reshape(depth * MOD_ROWS, 1, 6 * d)
    gains = norm_w.reshape(depth * 4, 1, d)
    rope_tabs = _rope_tables(seq)
    dec = jnp.broadcast_to(ret_decay.astype(F32)[:, :, :, None, None], (depth, 2, RET_HEADS, 8, LANES))
    router_pad = jnp.pad(router, ((0, 0), (0, 0), (0, LANES - N_EXPERTS)))

    xs = jnp.concatenate([x.reshape(n_lat, d), ctx.reshape(batch * n_ctx, d)], axis=0)
    h = _prologue(xs, mod, gains, seq, batch, tm_small)

    for layer in range(depth):
        last = layer == depth - 1
        is_moe = layer % 2 == 1
        i = layer // 2
        proj = _in_proj(h, w_in[layer].astype(BF16), tm_proj)
        ctx_r, ctx_c, states = _mixer(proj, dec[layer], conv_w[layer], None, None,
                                      batch=batch, seq_len=n_ctx, row0=n_lat)
        lat_r, lat_c = _mixer(proj, dec[layer], conv_w[layer], rope_tabs, states,
                              batch=batch, seq_len=seq, row0=0)
        n_rows = n_lat if last else t_all
        xs, h2 = _out_proj((lat_r, lat_c), (ctx_r, ctx_c), w_out[layer].astype(BF16), xs, mod, gains,
                           layer, seq, batch, n_rows, tm_small, F32 if is_moe else BF16)
        if not is_moe:
            xs, h = _dense_ffn(h2, ffn_w1[i].astype(BF16), ffn_w3[i].astype(BF16),
                               ffn_w2[i].astype(BF16), xs, mod, gains, layer, seq, batch, tm_ffn)
        else:
            rinfo = _router(h2, router_pad[i], tm_small)
            plan = _dispatch_plan(rinfo, tm_ffn)
            y2 = _moe_ffn(h2, plan, moe_w1[i].astype(BF16), moe_w3[i].astype(BF16),
                          moe_w2[i].astype(BF16), tm_ffn)
            xs, h = _moe_combine(y2, xs, mod, gains, layer, seq, batch, tm_ffn, not last)
    return xs.reshape(batch, seq, d)
```

```python
import functools

import jax
import jax.numpy as jnp
from jax import lax
from jax.experimental import pallas as pl
from jax.experimental.pallas import tpu as pltpu

RET_HEADS = 8
HEAD_DIM = 128
CHUNK = 128
CONV_WIDTH = 3
GRID_W = 64
ROPE_BASE = 10000.0
EPS = 1e-6
N_EXPERTS = 8
LANES = 128
MOD_ROWS = 8
VMEM_LIMIT = 56 * 1024 * 1024

F32 = jnp.float32
BF16 = jnp.bfloat16
U32 = jnp.uint32


def _params(semantics):
    return pltpu.CompilerParams(dimension_semantics=semantics, vmem_limit_bytes=VMEM_LIMIT)


def _pick_tile(target, *sizes):
    t = target
    while t > 8 and any(s % t for s in sizes):
        t //= 2
    assert all(s % t == 0 for s in sizes), (target, sizes)
    return t


def _silu(v):
    return v * jax.nn.sigmoid(v)


def _rms(v, gain):
    return v * lax.rsqrt(jnp.mean(v * v, axis=-1, keepdims=True) + EPS) * gain


def _dot(a, b):
    return jnp.dot(a, b, preferred_element_type=F32)


def _pack_halves(h):
    n = h.shape[1] // 2
    hb = h.astype(BF16).astype(F32)
    lo = lax.shift_right_logical(lax.bitcast_convert_type(hb[:, :n], U32), jnp.uint32(16))
    hi = lax.bitcast_convert_type(hb[:, n:], U32) & jnp.uint32(0xFFFF0000)
    return lo | hi


def _unpack_halves(u):
    lo = lax.bitcast_convert_type(lax.shift_left(u, jnp.uint32(16)), F32).astype(BF16)
    hi = lax.bitcast_convert_type(u & jnp.uint32(0xFFFF0000), F32).astype(BF16)
    return lo, hi


def _ada_kernel(c_ref, w_ref, b_ref, o_ref):
    a = _silu(c_ref[...])
    o_ref[...] = jnp.dot(a, w_ref[...], preferred_element_type=F32,
                         precision=lax.Precision.HIGHEST) + b_ref[...]


def _ada_table(c8, ada_w, ada_b):
    depth, d, n = ada_w.shape
    tn = _pick_tile(1024, n)
    return pl.pallas_call(
        _ada_kernel,
        grid=(depth, n // tn),
        in_specs=[pl.BlockSpec((MOD_ROWS, d), lambda l, j: (0, 0)),
                  pl.BlockSpec((None, d, tn), lambda l, j: (l, 0, j)),
                  pl.BlockSpec((None, 1, tn), lambda l, j: (l, 0, j))],
        out_specs=pl.BlockSpec((None, MOD_ROWS, tn), lambda l, j: (l, 0, j)),
        out_shape=jax.ShapeDtypeStruct((depth, MOD_ROWS, n), F32),
        compiler_params=_params(("arbitrary", "arbitrary")),
        name="ada_table",
    )(c8, ada_w, ada_b.reshape(depth, 1, n))


class _Rows:
    def __init__(self, tm, d, seq, batch, layer, n_prefetch=0):
        self.tm, self.d, self.layer = tm, d, layer
        self.tiles_per_batch = seq // tm
        self.batch = batch
        self.n_prefetch = n_prefetch

    def _wrap(self, fn):
        return lambda t, *_: fn(t)

    def tile(self, width=None):
        return pl.BlockSpec((self.tm, width or self.d), self._wrap(lambda t: (t, 0)))

    def mod(self, which, layer=None):
        layer = self.layer if layer is None else layer
        tpb, b = self.tiles_per_batch, self.batch
        return pl.BlockSpec(
            (None, 1, self.d),
            self._wrap(lambda t: (layer * MOD_ROWS + jnp.minimum(t // tpb, b), 0, which)))

    def gain(self, which, layer=None):
        layer = self.layer if layer is None else layer
        return pl.BlockSpec((None, 1, self.d), self._wrap(lambda t: (layer * 4 + which, 0, 0)))

    def next_layer(self):
        return [self.gain(0, self.layer + 1), self.mod(0, self.layer + 1), self.mod(1, self.layer + 1)]


def _prologue_kernel(x_ref, g_ref, sh_ref, sc_ref, h_ref):
    h_ref[...] = (_rms(x_ref[...], g_ref[...]) * (1.0 + sc_ref[...]) + sh_ref[...]).astype(h_ref.dtype)


def _prologue(xs, mod, gains, seq, batch, tm):
    t_rows, d = xs.shape
    rows = _Rows(tm, d, seq, batch, 0)
    return pl.pallas_call(
        _prologue_kernel,
        grid=(t_rows // tm,),
        in_specs=[rows.tile(), rows.gain(0), rows.mod(0), rows.mod(1)],
        out_specs=rows.tile(),
        out_shape=jax.ShapeDtypeStruct((t_rows, d), BF16),
        compiler_params=_params(("arbitrary",)),
        name="prologue_norm",
    )(xs, gains, mod, mod)


def _in_proj_kernel(h_ref, w_ref, o_ref, wb_ref):
    @pl.when(pl.program_id(1) == 0)
    def _():
        wb_ref[...] = w_ref[...].astype(BF16)

    o_ref[...] = _dot(h_ref[...], wb_ref[...]).astype(o_ref.dtype)


def _in_proj(h, w_in, layer, tm):
    t_rows, d = h.shape
    n = w_in.shape[2]
    tn = _pick_tile(1024, n)
    return pl.pallas_call(
        _in_proj_kernel,
        grid=(n // tn, t_rows // tm),
        in_specs=[pl.BlockSpec((tm, d), lambda j, i: (i, 0)),
                  pl.BlockSpec((None, d, tn), lambda j, i: (layer, 0, j))],
        out_specs=pl.BlockSpec((tm, tn), lambda j, i: (i, j)),
        out_shape=jax.ShapeDtypeStruct((t_rows, n), BF16),
        scratch_shapes=[pltpu.VMEM((d, tn), BF16)],
        compiler_params=_params(("arbitrary", "arbitrary")),
        name="in_proj",
    )(h, w_in)


def _swap32(t, lane_lo):
    return jnp.where(lane_lo, pltpu.roll(t, 96, 1), pltpu.roll(t, 32, 1))


def _mixer_kernel(*refs, seq_len, rope, state_in, state_out):
    it = iter(refs)
    q_ref, k_ref, v_ref, g_ref, b_ref, c_ref, u_ref = (next(it) for _ in range(7))
    dec_ref, cw_ref = next(it), next(it)
    cos_ref = sin_ref = st_ref = None
    if rope:
        cos_ref, sin_ref = next(it), next(it)
    if state_in:
        st_ref = next(it)
    ret_ref, conv_ref = next(it), next(it)
    sto_ref = next(it) if state_out else None
    qs, ks, sums, states, zpad = (next(it) for _ in range(5))

    n_chunks = seq_len // CHUNK
    unroll = min(4, n_chunks)
    blk = min(256, seq_len)

    lg = jnp.log1p(-jnp.exp2(dec_ref[...]))
    lgf = jnp.broadcast_to(lg[0, 0:1, :], (CHUNK, CHUNK))
    lgb = jnp.broadcast_to(lg[1, 0:1, :], (CHUNK, CHUNK))
    ii = lax.broadcasted_iota(jnp.int32, (CHUNK, CHUNK), 0).astype(F32)
    jj = lax.broadcasted_iota(jnp.int32, (CHUNK, CHUNK), 1).astype(F32)
    diff = ii - jj
    decay = jnp.where(diff >= 0, jnp.exp(lgf * jnp.maximum(diff, 0.0)),
                      jnp.exp(lgb * jnp.maximum(-diff, 0.0)))
    wq_f = jnp.exp(lgf * (ii + 1.0))
    wq_b = jnp.exp(lgb * (CHUNK - ii))
    wk_f = jnp.exp(lgf * (CHUNK - 1.0 - ii))
    wk_b = jnp.exp(lgb * ii)
    gc_f = jnp.exp(lgf * CHUNK)
    gc_b = jnp.exp(lgb * CHUNK)
    k_scale = HEAD_DIM ** -0.5

    lane = lax.broadcasted_iota(jnp.int32, (blk, LANES), 1)
    lane_lo = (lane % 64) < 32
    for r0 in range(0, seq_len, blk):
        rows = pl.ds(r0, blk)
        q = q_ref[rows, :].astype(F32)
        k = k_ref[rows, :].astype(F32)
        if rope:
            cs, sn = cos_ref[rows, :], sin_ref[rows, :]
            q = q * cs + _swap32(q, lane_lo) * sn
            k = k * cs + _swap32(k, lane_lo) * sn
        qs[rows, :] = q.astype(BF16)
        ks[rows, :] = (k * k_scale).astype(BF16)

    def chunk(n):
        return pl.ds(pl.multiple_of(n * CHUNK, CHUNK), CHUNK)

    def summary_step(n, carry):
        kf = ks[chunk(n), :].astype(F32)
        kw = jnp.concatenate([kf * wk_f, kf * wk_b], axis=1).T.astype(BF16)
        sums[n] = _dot(kw, v_ref[chunk(n), :])
        return carry

    lax.fori_loop(0, n_chunks, summary_step, 0, unroll=unroll)

    def fwd_scan(n, state):
        states[n, pl.ds(0, HEAD_DIM), :] = state.astype(BF16)
        return state * gc_f + sums[n, pl.ds(0, HEAD_DIM), :]

    def bwd_scan(i, state):
        n = n_chunks - 1 - i
        states[n, pl.ds(HEAD_DIM, HEAD_DIM), :] = state.astype(BF16)
        return state * gc_b + sums[n, pl.ds(HEAD_DIM, HEAD_DIM), :]

    zero = jnp.zeros((HEAD_DIM, HEAD_DIM), F32)
    s_f_final = lax.fori_loop(0, n_chunks, fwd_scan, st_ref[0] if state_in else zero)
    s_b_final = lax.fori_loop(0, n_chunks, bwd_scan, st_ref[1] if state_in else zero)
    if state_out:
        sto_ref[0] = s_f_final
        sto_ref[1] = s_b_final

    def out_step(n, carry):
        qn = qs[chunk(n), :]
        vn = v_ref[chunk(n), :]
        scores = lax.dot_general(qn, ks[chunk(n), :], (((1,), (1,)), ((), ())),
                                 preferred_element_type=F32)
        qf = qn.astype(F32)
        qw = jnp.concatenate([qf * wq_f, qf * wq_b], axis=1).astype(BF16)
        y = _dot((scores * decay).astype(BF16), vn) + _dot(qw, states[n])
        mu = jnp.mean(y, axis=-1, keepdims=True)
        yc = y - mu
        yn = yc * lax.rsqrt(jnp.mean(yc * yc, axis=-1, keepdims=True) + EPS)
        gate = g_ref[chunk(n), :].astype(F32)
        ret_ref[chunk(n), :] = (_silu(gate) * yn).astype(ret_ref.dtype)
        return carry

    lax.fori_loop(0, n_chunks, out_step, 0, unroll=unroll)

    zpad[pl.ds(0, 8), :] = jnp.zeros((8, LANES), F32)
    zpad[pl.ds(8 + seq_len, 8), :] = jnp.zeros((8, LANES), F32)
    for r0 in range(0, seq_len, blk):
        rows = pl.ds(r0, blk)
        zpad[pl.ds(8 + r0, blk), :] = c_ref[rows, :].astype(F32) * u_ref[rows, :].astype(F32)
    cw = cw_ref[...]
    for r0 in range(0, seq_len, blk):
        rows = pl.ds(r0, blk)
        conv = (cw[0:1, :] * zpad[pl.ds(7 + r0, blk), :] + cw[1:2, :] * zpad[pl.ds(8 + r0, blk), :]
                + cw[2:3, :] * zpad[pl.ds(9 + r0, blk), :])
        y = b_ref[rows, :].astype(F32) * conv
        conv_ref[rows, :] = (y * lax.rsqrt(jnp.mean(y * y, axis=-1, keepdims=True) + EPS)
                             ).astype(conv_ref.dtype)


def _mixer(proj, dec, conv_w, rope_tabs, states, *, batch, seq_len, row0):
    d_half = RET_HEADS * HEAD_DIM
    rb0 = row0 // seq_len
    rope = rope_tabs is not None
    state_in = states is not None
    n_chunks = seq_len // CHUNK

    def col(cb):
        return pl.BlockSpec((seq_len, LANES), lambda b, j: (rb0 + b, cb * RET_HEADS + j))

    in_specs = [col(i) for i in range(7)]
    args = [proj] * 7
    in_specs += [pl.BlockSpec((2, None, 8, LANES), lambda b, j: (0, j, 0, 0)),
                 pl.BlockSpec((CONV_WIDTH, LANES), lambda b, j: (0, j))]
    args += [dec, conv_w]
    if rope:
        in_specs += [pl.BlockSpec((seq_len, LANES), lambda b, j: (0, 0))] * 2
        args += list(rope_tabs)
    state_spec = pl.BlockSpec((None, None, 2, HEAD_DIM, HEAD_DIM), lambda b, j: (b, j, 0, 0, 0))
    if state_in:
        in_specs.append(state_spec)
        args.append(states)
    out_spec = pl.BlockSpec((seq_len, LANES), lambda b, j: (b, j))
    out_specs = [out_spec, out_spec]
    out_shape = [jax.ShapeDtypeStruct((batch * seq_len, d_half), BF16)] * 2
    if not state_in:
        out_specs.append(state_spec)
        out_shape.append(jax.ShapeDtypeStruct((batch, RET_HEADS, 2, HEAD_DIM, HEAD_DIM), F32))
    return pl.pallas_call(
        functools.partial(_mixer_kernel, seq_len=seq_len, rope=rope, state_in=state_in,
                          state_out=not state_in),
        grid=(batch, RET_HEADS),
        in_specs=in_specs,
        out_specs=out_specs,
        out_shape=out_shape,
        scratch_shapes=[pltpu.VMEM((seq_len, LANES), BF16), pltpu.VMEM((seq_len, LANES), BF16),
                        pltpu.VMEM((n_chunks, 2 * HEAD_DIM, HEAD_DIM), F32),
                        pltpu.VMEM((n_chunks, 2 * HEAD_DIM, HEAD_DIM), BF16),
                        pltpu.VMEM((seq_len + 16, LANES), F32)],
        compiler_params=_params(("arbitrary", "arbitrary")),
        name="mixer_latent" if state_in else "mixer_context",
    )(*args)


def _residual_next(y, x, gain_post, gate, nxt):
    x_new = x + gate * _rms(y, gain_post)
    if nxt is None:
        return x_new, None
    gain_n, shift_n, scale_n = nxt
    return x_new, _rms(x_new, gain_n) * (1.0 + scale_n) + shift_n


def _route_top2(logits):
    lane = lax.broadcasted_iota(jnp.int32, logits.shape, 1)
    neg = jnp.float32(-jnp.inf)
    l1 = jnp.where(lane < N_EXPERTS, logits, neg)
    m1 = jnp.max(l1, axis=-1, keepdims=True)
    i1 = jnp.min(jnp.where(l1 == m1, lane, LANES), axis=-1, keepdims=True)
    l2 = jnp.where(lane == i1, neg, l1)
    m2 = jnp.max(l2, axis=-1, keepdims=True)
    i2 = jnp.min(jnp.where(l2 == m2, lane, LANES), axis=-1, keepdims=True)
    e2 = jnp.exp(m2 - m1)
    p1 = 1.0 / (1.0 + e2)
    p2 = e2 / (1.0 + e2)
    return jnp.where(lane == 0, i1.astype(F32),
                     jnp.where(lane == 1, i2.astype(F32),
                               jnp.where(lane == 2, p1, jnp.where(lane == 3, p2, 0.0))))


def _out_proj_kernel(*refs, n_lat_tiles, moe):
    (lr_ref, lc_ref, cr_ref, cc_ref, w_ref, x_ref, gp_ref, gt_ref, gn_ref, sh_ref, sc_ref) = refs[:11]
    if moe:
        rt_ref, xo_ref, hp_ref, ri_ref = refs[11:]
    else:
        xo_ref, h_ref = refs[11:]
    half = lr_ref.shape[1]

    def finish(mr_ref, mc_ref):
        y = _dot(mr_ref[...], w_ref[pl.ds(0, half), :]) + _dot(mc_ref[...], w_ref[pl.ds(half, half), :])
        x_new, h = _residual_next(y, x_ref[...], gp_ref[...], gt_ref[...],
                                  (gn_ref[...], sh_ref[...], sc_ref[...]))
        xo_ref[...] = x_new
        if moe:
            ri_ref[...] = _route_top2(jnp.dot(h, rt_ref[...], preferred_element_type=F32,
                                              precision=lax.Precision.HIGHEST))
            hp_ref[...] = _pack_halves(h)
        else:
            h_ref[...] = h.astype(h_ref.dtype)

    @pl.when(pl.program_id(0) < n_lat_tiles)
    def _():
        finish(lr_ref, lc_ref)

    @pl.when(pl.program_id(0) >= n_lat_tiles)
    def _():
        finish(cr_ref, cc_ref)


def _out_proj(mix_lat, mix_ctx, w_out, router_pad, xs, mod, gains, layer, seq, batch, n_rows, tm):
    d = xs.shape[1]
    half = mix_lat[0].shape[1]
    moe = router_pad is not None
    rows = _Rows(tm, d, seq, batch, layer)
    n_lat_tiles = mix_lat[0].shape[0] // tm
    n_ctx_tiles = mix_ctx[0].shape[0] // tm
    lat_spec = pl.BlockSpec((tm, half), lambda t: (jnp.minimum(t, n_lat_tiles - 1), 0))
    ctx_spec = pl.BlockSpec((tm, half), lambda t: (jnp.clip(t - n_lat_tiles, 0, n_ctx_tiles - 1), 0))
    in_specs = [lat_spec, lat_spec, ctx_spec, ctx_spec, pl.BlockSpec((d, d), lambda t: (0, 0)),
                rows.tile(), rows.gain(1), rows.mod(2), rows.gain(2), rows.mod(3), rows.mod(4)]
    args = [mix_lat[0], mix_lat[1], mix_ctx[0], mix_ctx[1], w_out, xs, gains, mod, gains, mod, mod]
    out_specs = [rows.tile()]
    out_shape = [jax.ShapeDtypeStruct((n_rows, d), F32)]
    if moe:
        in_specs.append(pl.BlockSpec((d, LANES), lambda t: (0, 0)))
        args.append(router_pad)
        out_specs += [rows.tile(d // 2), rows.tile(LANES)]
        out_shape += [jax.ShapeDtypeStruct((n_rows, d // 2), U32),
                      jax.ShapeDtypeStruct((n_rows, LANES), F32)]
    else:
        out_specs.append(rows.tile())
        out_shape.append(jax.ShapeDtypeStruct((n_rows, d), BF16))
    return pl.pallas_call(
        functools.partial(_out_proj_kernel, n_lat_tiles=n_lat_tiles, moe=moe),
        grid=(n_rows // tm,),
        in_specs=in_specs,
        out_specs=out_specs,
        out_shape=out_shape,
        compiler_params=_params(("arbitrary",)),
        name="out_proj",
    )(*args)


def _swiglu_step(h, w1_ref, w3_ref, w2_ref, y_ref):
    a = (_silu(_dot(h, w1_ref[...].astype(BF16))) * _dot(h, w3_ref[...].astype(BF16))).astype(BF16)
    y_ref[...] += _dot(a, w2_ref[...].astype(BF16))


def _ffn_kernel(h_ref, w1_ref, w3_ref, w2_ref, y_ref):
    @pl.when(pl.program_id(1) == 0)
    def _():
        y_ref[...] = jnp.zeros(y_ref.shape, F32)

    _swiglu_step(h_ref[...], w1_ref, w3_ref, w2_ref, y_ref)


def _dense_ffn(h, w1, w3, w2, idx, tm):
    t_rows, d = h.shape
    ff = w1.shape[2]
    tf = _pick_tile(256, ff)
    return pl.pallas_call(
        _ffn_kernel,
        grid=(t_rows // tm, ff // tf),
        in_specs=[pl.BlockSpec((tm, d), lambda t, f: (t, 0)),
                  pl.BlockSpec((None, d, tf), lambda t, f: (idx, 0, f)),
                  pl.BlockSpec((None, d, tf), lambda t, f: (idx, 0, f)),
                  pl.BlockSpec((None, tf, d), lambda t, f: (idx, f, 0))],
        out_specs=pl.BlockSpec((tm, d), lambda t, f: (t, 0)),
        out_shape=jax.ShapeDtypeStruct((t_rows, d), F32),
        compiler_params=_params(("arbitrary", "arbitrary")),
        name="dense_ffn",
    )(h, w1, w3, w2)


def _epilogue_kernel(*refs, emit_next):
    if emit_next:
        y_ref, x_ref, gp_ref, gt_ref, gn_ref, sh_ref, sc_ref, xo_ref, hn_ref = refs
        nxt = (gn_ref[...], sh_ref[...], sc_ref[...])
    else:
        y_ref, x_ref, gp_ref, gt_ref, xo_ref = refs
        nxt = None
    x_new, hn = _residual_next(y_ref[...], x_ref[...], gp_ref[...], gt_ref[...], nxt)
    xo_ref[...] = x_new
    if emit_next:
        hn_ref[...] = hn.astype(hn_ref.dtype)


def _ffn_epilogue(y, xs, mod, gains, layer, seq, batch, tm):
    t_rows, d = xs.shape
    rows = _Rows(tm, d, seq, batch, layer)
    return pl.pallas_call(
        functools.partial(_epilogue_kernel, emit_next=True),
        grid=(t_rows // tm,),
        in_specs=[rows.tile(), rows.tile(), rows.gain(3), rows.mod(5)] + rows.next_layer(),
        out_specs=[rows.tile(), rows.tile()],
        out_shape=[jax.ShapeDtypeStruct((t_rows, d), F32), jax.ShapeDtypeStruct((t_rows, d), BF16)],
        compiler_params=_params(("arbitrary",)),
        name="ffn_epilogue",
    )(y, xs, gains, mod, gains, mod, mod)


def _dispatch_plan(rinfo, tm):
    t_rows = rinfo.shape[0]
    n_flat = 2 * t_rows
    n_tiles = n_flat // tm + N_EXPERTS
    e_flat = rinfo[:, 0:2].astype(jnp.int32).T.reshape(-1)
    onehot = (e_flat[:, None] == jnp.arange(N_EXPERTS, dtype=jnp.int32)[None, :]).astype(jnp.int32)
    csum = jnp.cumsum(onehot, axis=0)
    counts = csum[-1]
    padded = (counts + tm - 1) // tm * tm
    ends_p = jnp.cumsum(padded)
    starts_p = ends_p - padded
    pos = jnp.sum(onehot * (csum - 1 + starts_p[None, :]), axis=1).astype(jnp.int32)
    tile_start = jnp.arange(n_tiles, dtype=jnp.int32) * tm
    tile_e = jnp.minimum(jnp.sum(tile_start[:, None] >= ends_p[None, :], axis=1), N_EXPERTS - 1)
    tile_nv = jnp.where(tile_start < ends_p[-1],
                        jnp.clip(counts[tile_e] - (tile_start - starts_p[tile_e]), 0, tm), 0)
    src_tok = jnp.zeros(((n_tiles + 1) * tm,), jnp.int32).at[pos].set(
        jnp.arange(n_flat, dtype=jnp.int32) % t_rows, unique_indices=True)
    last_used = jnp.maximum(jnp.sum(tile_nv > 0) - 1, 0)
    tile_e = jnp.where(tile_nv > 0, tile_e, tile_e[last_used])
    return tile_e.astype(jnp.int32), tile_nv.astype(jnp.int32), src_tok, pos


def _moe_kernel(te_ref, nv_ref, src_ref, h_hbm, w1_ref, w3_ref, w2_ref, y_ref, hbuf, hb, sem,
                *, tm, rows_per_step):
    t = pl.program_id(0)
    f = pl.program_id(1)
    slot = t % 2
    half = hbuf.shape[2]

    def gather_row(row, dst_slot, r):
        pltpu.make_async_copy(h_hbm.at[pl.ds(src_ref[row], 1), :],
                              hbuf.at[dst_slot, pl.ds(r, 1), :], sem.at[dst_slot]).start()

    valid = nv_ref[t] > 0
    prev_valid = nv_ref[jnp.maximum(t - 1, 0)] > 0

    @pl.when((f == 0) & (t == 0))
    def _():
        def issue(r, carry):
            gather_row(r, 0, r)
            return carry

        lax.fori_loop(0, tm, issue, 0)

    @pl.when((f == 0) & ((t == 0) | prev_valid))
    def _():
        pltpu.make_async_copy(h_hbm.at[pl.ds(0, tm), :], hbuf.at[slot], sem.at[slot]).wait()

    @pl.when(f == 0)
    def _():
        y_ref[...] = jnp.zeros(y_ref.shape, F32)

    @pl.when(valid)
    def _():
        @pl.when(f == 0)
        def _():
            lo, hi = _unpack_halves(hbuf[slot])
            hb[:, pl.ds(0, half)] = lo
            hb[:, pl.ds(half, half)] = hi

        nxt = (t + 1) * tm + f * rows_per_step
        for r in range(rows_per_step):
            gather_row(nxt + r, 1 - slot, f * rows_per_step + r)
        _swiglu_step(hb[...], w1_ref, w3_ref, w2_ref, y_ref)


def _moe_ffn(h_packed, plan, w1, w3, w2, idx, tm):
    tile_e, tile_nv, src_tok, _ = plan
    t_rows, half = h_packed.shape
    d = 2 * half
    ff = w1.shape[3]
    tf = _pick_tile(256, ff)
    n_tiles = tile_e.shape[0]
    n_f = ff // tf
    assert tm % n_f == 0

    def f_eff(t, f, nv):
        return jnp.where(nv[t] > 0, f, n_f - 1)

    grid_spec = pltpu.PrefetchScalarGridSpec(
        num_scalar_prefetch=3,
        grid=(n_tiles, n_f),
        in_specs=[pl.BlockSpec(memory_space=pl.ANY),
                  pl.BlockSpec((None, None, d, tf), lambda t, f, te, nv, s: (idx, te[t], 0, f_eff(t, f, nv))),
                  pl.BlockSpec((None, None, d, tf), lambda t, f, te, nv, s: (idx, te[t], 0, f_eff(t, f, nv))),
                  pl.BlockSpec((None, None, tf, d), lambda t, f, te, nv, s: (idx, te[t], f_eff(t, f, nv), 0))],
        out_specs=pl.BlockSpec((tm, d), lambda t, f, te, nv, s: (t, 0)),
        scratch_shapes=[pltpu.VMEM((2, tm, half), U32), pltpu.VMEM((tm, d), BF16),
                        pltpu.SemaphoreType.DMA((2,))])
    return pl.pallas_call(
        functools.partial(_moe_kernel, tm=tm, rows_per_step=tm // n_f),
        grid_spec=grid_spec,
        out_shape=jax.ShapeDtypeStruct((n_tiles * tm, d), F32),
        compiler_params=_params(("arbitrary", "arbitrary")),
        name="moe_ffn",
    )(tile_e, tile_nv, src_tok, h_packed, w1, w3, w2)


def _combine_kernel(*refs, tm, t_rows, emit_next):
    pos_ref, y_hbm, ri_ref, x_ref, gp_ref, gt_ref = refs[:6]
    if emit_next:
        gn_ref, sh_ref, sc_ref, xo_ref, hn_ref, ybuf, sem = refs[6:]
        nxt = (gn_ref[...], sh_ref[...], sc_ref[...])
    else:
        xo_ref, ybuf, sem = refs[6:]
        nxt = None
    i = pl.program_id(0)
    n = pl.num_programs(0)
    slot = i % 2

    def gather_row(tile, dst_slot, r):
        tok = tile * tm + r % tm
        src = pos_ref[(r // tm) * t_rows + tok]
        pltpu.make_async_copy(y_hbm.at[pl.ds(src, 1), :], ybuf.at[dst_slot, pl.ds(r, 1), :],
                              sem.at[dst_slot]).start()

    def wait(s):
        pltpu.make_async_copy(y_hbm.at[pl.ds(0, 2 * tm), :], ybuf.at[s], sem.at[s]).wait()

    @pl.when(i == 0)
    def _():
        def issue(r, carry):
            gather_row(0, 0, r)
            return carry

        lax.fori_loop(0, 2 * tm, issue, 0)

    wait(slot)
    nxt_tile = jnp.minimum(i + 1, n - 1)
    for r in range(2 * tm):
        gather_row(nxt_tile, 1 - slot, r)
    ri = ri_ref[...]
    y = ri[:, 2:3] * ybuf[slot, pl.ds(0, tm), :] + ri[:, 3:4] * ybuf[slot, pl.ds(tm, tm), :]
    x_new, hn = _residual_next(y, x_ref[...], gp_ref[...], gt_ref[...], nxt)
    xo_ref[...] = x_new
    if emit_next:
        hn_ref[...] = hn.astype(hn_ref.dtype)

    @pl.when(i == n - 1)
    def _():
        wait(1 - slot)


def _moe_combine(y_sorted, pos, rinfo, xs, mod, gains, layer, seq, batch, tm, emit_next):
    t_rows, d = xs.shape
    rows = _Rows(tm, d, seq, batch, layer)
    in_specs = [pl.BlockSpec(memory_space=pl.ANY), rows.tile(LANES), rows.tile(), rows.gain(3), rows.mod(5)]
    args = [y_sorted, rinfo, xs, gains, mod]
    out_specs = [rows.tile()]
    out_shape = [jax.ShapeDtypeStruct((t_rows, d), F32)]
    if emit_next:
        in_specs += rows.next_layer()
        args += [gains, mod, mod]
        out_specs.append(rows.tile())
        out_shape.append(jax.ShapeDtypeStruct((t_rows, d), BF16))
    grid_spec = pltpu.PrefetchScalarGridSpec(
        num_scalar_prefetch=1,
        grid=(t_rows // tm,),
        in_specs=in_specs,
        out_specs=out_specs,
        scratch_shapes=[pltpu.VMEM((2, 2 * tm, d), F32), pltpu.SemaphoreType.DMA((2,))])
    out = pl.pallas_call(
        functools.partial(_combine_kernel, tm=tm, t_rows=t_rows, emit_next=emit_next),
        grid_spec=grid_spec,
        out_shape=out_shape,
        compiler_params=_params(("arbitrary",)),
        name="moe_combine",
    )(pos, *args)
    return (out[0], out[1]) if emit_next else (out[0], None)


def _rope_tables(seq):
    pos = jnp.arange(seq, dtype=jnp.int32)
    row = (pos // GRID_W).astype(F32)
    col = (pos % GRID_W).astype(F32)
    n_freq = HEAD_DIM // 4
    freqs = ROPE_BASE ** (-jnp.arange(n_freq, dtype=F32) / n_freq)
    ang_r = row[:, None] * freqs[None, :]
    ang_c = col[:, None] * freqs[None, :]
    cos = jnp.concatenate([jnp.cos(ang_r), jnp.cos(ang_r), jnp.cos(ang_c), jnp.cos(ang_c)], axis=-1)
    sin = jnp.concatenate([-jnp.sin(ang_r), jnp.sin(ang_r), -jnp.sin(ang_c), jnp.sin(ang_c)], axis=-1)
    return cos, sin


def kernel(x, c, ctx, c_ctx, w_in, w_out, conv_w, ret_decay, ada_w, ada_b, norm_w, ffn_w1, ffn_w3,
           ffn_w2, router, moe_w1, moe_w3, moe_w2):
    batch, seq, d = x.shape
    n_ctx = ctx.shape[1]
    depth = w_in.shape[0]
    n_lat = batch * seq
    assert d == 2 * RET_HEADS * HEAD_DIM and batch + 1 <= MOD_ROWS
    assert seq % CHUNK == 0 and n_ctx % CHUNK == 0 and n_lat % n_ctx == 0

    tm_small = _pick_tile(256, seq, batch * n_ctx)
    tm_big = _pick_tile(1024, seq, batch * n_ctx)

    c8 = jnp.zeros((MOD_ROWS, d), F32).at[:batch].set(c).at[batch].set(c_ctx)
    mod = _ada_table(c8, ada_w, ada_b).reshape(depth * MOD_ROWS, 1, 6 * d)
    gains = norm_w.reshape(depth * 4, 1, d)
    rope_tabs = _rope_tables(seq)
    dec = jnp.broadcast_to(ret_decay.astype(F32)[:, :, :, None, None], (depth, 2, RET_HEADS, 8, LANES))
    router_pad = jnp.pad(router, ((0, 0), (0, 0), (0, LANES - N_EXPERTS)))

    xs = jnp.concatenate([x.reshape(n_lat, d), ctx.reshape(batch * n_ctx, d)], axis=0)
    h = _prologue(xs, mod, gains, seq, batch, tm_small)

    for layer in range(depth):
        last = layer == depth - 1
        is_moe = layer % 2 == 1
        i = layer // 2
        proj = _in_proj(h, w_in, layer, tm_big)
        ctx_r, ctx_c, states = _mixer(proj, dec[layer], conv_w[layer], None, None,
                                      batch=batch, seq_len=n_ctx, row0=n_lat)
        lat_r, lat_c = _mixer(proj, dec[layer], conv_w[layer], rope_tabs, states,
                              batch=batch, seq_len=seq, row0=0)
        n_rows = n_lat if last else xs.shape[0]
        out = _out_proj((lat_r, lat_c), (ctx_r, ctx_c), w_out[layer].astype(BF16),
                        router_pad[i] if is_moe else None, xs, mod, gains, layer, seq, batch,
                        n_rows, tm_small)
        if not is_moe:
            xs, h2 = out
            y = _dense_ffn(h2, ffn_w1, ffn_w3, ffn_w2, i, tm_big)
            xs, h = _ffn_epilogue(y, xs, mod, gains, layer, seq, batch, tm_small)
        else:
            xs, h_packed, rinfo = out
            plan = _dispatch_plan(rinfo, tm_big)
            y_sorted = _moe_ffn(h_packed, plan, moe_w1, moe_w3, moe_w2, i, tm_big)
            xs, h = _moe_combine(y_sorted, plan[3], rinfo, xs, mod, gains, layer, seq, batch,
                                 tm_small, not last)
    return xs.reshape(batch, seq, d)
```

```python
import functools

import jax
import jax.numpy as jnp
from jax import lax
from jax.experimental import pallas as pl
from jax.experimental.pallas import tpu as pltpu

RET_HEADS = 8
HEAD_DIM = 128
CHUNK = 128
CONV_WIDTH = 3
GRID_W = 64
ROPE_BASE = 10000.0
EPS = 1e-6
N_EXPERTS = 8
LANES = 128
MOD_ROWS = 8
VMEM_LIMIT = 56 * 1024 * 1024

F32 = jnp.float32
BF16 = jnp.bfloat16
U32 = jnp.uint32


def _params(semantics):
    return pltpu.CompilerParams(dimension_semantics=semantics, vmem_limit_bytes=VMEM_LIMIT)


def _pick_tile(target, *sizes):
    t = target
    while t > 8 and any(s % t for s in sizes):
        t //= 2
    assert all(s % t == 0 for s in sizes), (target, sizes)
    return t


def _silu(v):
    return v * jax.nn.sigmoid(v)


def _rms(v, gain):
    return v * lax.rsqrt(jnp.mean(v * v, axis=-1, keepdims=True) + EPS) * gain


def _dot(a, b):
    return jnp.dot(a, b, preferred_element_type=F32)


def _pack_halves(h):
    n = h.shape[1] // 2
    hb = h.astype(BF16).astype(F32)
    lo = lax.shift_right_logical(lax.bitcast_convert_type(hb[:, :n], U32), jnp.uint32(16))
    hi = lax.bitcast_convert_type(hb[:, n:], U32) & jnp.uint32(0xFFFF0000)
    return lo | hi


def _unpack_halves(u):
    lo = lax.bitcast_convert_type(lax.shift_left(u, jnp.uint32(16)), F32).astype(BF16)
    hi = lax.bitcast_convert_type(u & jnp.uint32(0xFFFF0000), F32).astype(BF16)
    return lo, hi


def _ada_kernel(c_ref, w_ref, b_ref, o_ref):
    a = _silu(c_ref[...])
    o_ref[...] = jnp.dot(a, w_ref[...], preferred_element_type=F32,
                         precision=lax.Precision.HIGHEST) + b_ref[...]


def _ada_table(c8, ada_w, ada_b):
    depth, d, n = ada_w.shape
    tn = _pick_tile(1024, n)
    return pl.pallas_call(
        _ada_kernel,
        grid=(depth, n // tn),
        in_specs=[pl.BlockSpec((MOD_ROWS, d), lambda l, j: (0, 0)),
                  pl.BlockSpec((None, d, tn), lambda l, j: (l, 0, j)),
                  pl.BlockSpec((None, 1, tn), lambda l, j: (l, 0, j))],
        out_specs=pl.BlockSpec((None, MOD_ROWS, tn), lambda l, j: (l, 0, j)),
        out_shape=jax.ShapeDtypeStruct((depth, MOD_ROWS, n), F32),
        compiler_params=_params(("arbitrary", "arbitrary")),
        name="ada_table",
    )(c8, ada_w, ada_b.reshape(depth, 1, n))


class _Rows:
    def __init__(self, tm, d, seq, batch, layer, n_prefetch=0):
        self.tm, self.d, self.layer = tm, d, layer
        self.tiles_per_batch = seq // tm
        self.batch = batch
        self.n_prefetch = n_prefetch

    def _wrap(self, fn):
        return lambda t, *_: fn(t)

    def tile(self, width=None):
        return pl.BlockSpec((self.tm, width or self.d), self._wrap(lambda t: (t, 0)))

    def mod(self, which, layer=None):
        layer = self.layer if layer is None else layer
        tpb, b = self.tiles_per_batch, self.batch
        return pl.BlockSpec(
            (None, 1, self.d),
            self._wrap(lambda t: (layer * MOD_ROWS + jnp.minimum(t // tpb, b), 0, which)))

    def gain(self, which, layer=None):
        layer = self.layer if layer is None else layer
        return pl.BlockSpec((None, 1, self.d), self._wrap(lambda t: (layer * 4 + which, 0, 0)))

    def next_layer(self):
        return [self.gain(0, self.layer + 1), self.mod(0, self.layer + 1), self.mod(1, self.layer + 1)]


def _prologue_kernel(x_ref, g_ref, sh_ref, sc_ref, h_ref):
    h_ref[...] = (_rms(x_ref[...], g_ref[...]) * (1.0 + sc_ref[...]) + sh_ref[...]).astype(h_ref.dtype)


def _prologue(xs, mod, gains, seq, batch, tm):
    t_rows, d = xs.shape
    rows = _Rows(tm, d, seq, batch, 0)
    return pl.pallas_call(
        _prologue_kernel,
        grid=(t_rows // tm,),
        in_specs=[rows.tile(), rows.gain(0), rows.mod(0), rows.mod(1)],
        out_specs=rows.tile(),
        out_shape=jax.ShapeDtypeStruct((t_rows, d), BF16),
        compiler_params=_params(("arbitrary",)),
        name="prologue_norm",
    )(xs, gains, mod, mod)


def _in_proj_kernel(h_ref, w_ref, o_ref, wb_ref):
    @pl.when(pl.program_id(1) == 0)
    def _():
        wb_ref[...] = w_ref[...].astype(BF16)

    o_ref[...] = _dot(h_ref[...], wb_ref[...]).astype(o_ref.dtype)


def _in_proj(h, w_in, layer, tm):
    t_rows, d = h.shape
    n = w_in.shape[2]
    tn = _pick_tile(1024, n)
    return pl.pallas_call(
        _in_proj_kernel,
        grid=(n // tn, t_rows // tm),
        in_specs=[pl.BlockSpec((tm, d), lambda j, i: (i, 0)),
                  pl.BlockSpec((None, d, tn), lambda j, i: (layer, 0, j))],
        out_specs=pl.BlockSpec((tm, tn), lambda j, i: (i, j)),
        out_shape=jax.ShapeDtypeStruct((t_rows, n), BF16),
        scratch_shapes=[pltpu.VMEM((d, tn), BF16)],
        compiler_params=_params(("arbitrary", "arbitrary")),
        name="in_proj",
    )(h, w_in)


def _swap32(t, lane_lo):
    return jnp.where(lane_lo, pltpu.roll(t, 96, 1), pltpu.roll(t, 32, 1))


def _mixer_kernel(*refs, seq_len, rope, state_in, state_out):
    it = iter(refs)
    q_ref, k_ref, v_ref, g_ref, b_ref, c_ref, u_ref = (next(it) for _ in range(7))
    dec_ref, cw_ref = next(it), next(it)
    cos_ref = sin_ref = st_ref = None
    if rope:
        cos_ref, sin_ref = next(it), next(it)
    if state_in:
        st_ref = next(it)
    ret_ref, conv_ref = next(it), next(it)
    sto_ref = next(it) if state_out else None
    qs, ks, sums, states, zpad = (next(it) for _ in range(5))

    n_chunks = seq_len // CHUNK
    unroll = min(4, n_chunks)
    blk = min(256, seq_len)

    lg = jnp.log1p(-jnp.exp2(dec_ref[...]))
    lgf = jnp.broadcast_to(lg[0, 0:1, :], (CHUNK, CHUNK))
    lgb = jnp.broadcast_to(lg[1, 0:1, :], (CHUNK, CHUNK))
    ii = lax.broadcasted_iota(jnp.int32, (CHUNK, CHUNK), 0).astype(F32)
    jj = lax.broadcasted_iota(jnp.int32, (CHUNK, CHUNK), 1).astype(F32)
    diff = ii - jj
    decay = jnp.where(diff >= 0, jnp.exp(lgf * jnp.maximum(diff, 0.0)),
                      jnp.exp(lgb * jnp.maximum(-diff, 0.0)))
    wq_f = jnp.exp(lgf * (ii + 1.0))
    wq_b = jnp.exp(lgb * (CHUNK - ii))
    wk_f = jnp.exp(lgf * (CHUNK - 1.0 - ii))
    wk_b = jnp.exp(lgb * ii)
    gc_f = jnp.exp(lgf * CHUNK)
    gc_b = jnp.exp(lgb * CHUNK)
    k_scale = HEAD_DIM ** -0.5

    lane = lax.broadcasted_iota(jnp.int32, (blk, LANES), 1)
    lane_lo = (lane % 64) < 32
    for r0 in range(0, seq_len, blk):
        rows = pl.ds(r0, blk)
        q = q_ref[rows, :].astype(F32)
        k = k_ref[rows, :].astype(F32)
        if rope:
            cs, sn = cos_ref[rows, :], sin_ref[rows, :]
            q = q * cs + _swap32(q, lane_lo) * sn
            k = k * cs + _swap32(k, lane_lo) * sn
        qs[rows, :] = q.astype(BF16)
        ks[rows, :] = (k * k_scale).astype(BF16)

    def chunk(n):
        return pl.ds(pl.multiple_of(n * CHUNK, CHUNK), CHUNK)

    def summary_step(n, carry):
        kf = ks[chunk(n), :].astype(F32)
        kw = jnp.concatenate([kf * wk_f, kf * wk_b], axis=1).T.astype(BF16)
        sums[n] = _dot(kw, v_ref[chunk(n), :])
        return carry

    lax.fori_loop(0, n_chunks, summary_step, 0, unroll=unroll)

    def fwd_scan(n, state):
        states[n, pl.ds(0, HEAD_DIM), :] = state.astype(BF16)
        return state * gc_f + sums[n, pl.ds(0, HEAD_DIM), :]

    def bwd_scan(i, state):
        n = n_chunks - 1 - i
        states[n, pl.ds(HEAD_DIM, HEAD_DIM), :] = state.astype(BF16)
        return state * gc_b + sums[n, pl.ds(HEAD_DIM, HEAD_DIM), :]

    zero = jnp.zeros((HEAD_DIM, HEAD_DIM), F32)
    s_f_final = lax.fori_loop(0, n_chunks, fwd_scan, st_ref[0] if state_in else zero)
    s_b_final = lax.fori_loop(0, n_chunks, bwd_scan, st_ref[1] if state_in else zero)
    if state_out:
        sto_ref[0] = s_f_final
        sto_ref[1] = s_b_final

    def out_step(n, carry):
        qn = qs[chunk(n), :]
        vn = v_ref[chunk(n), :]
        scores = lax.dot_general(qn, ks[chunk(n), :], (((1,), (1,)), ((), ())),
                                 preferred_element_type=F32)
        qf = qn.astype(F32)
        qw = jnp.concatenate([qf * wq_f, qf * wq_b], axis=1).astype(BF16)
        y = _dot((scores * decay).astype(BF16), vn) + _dot(qw, states[n])
        mu = jnp.mean(y, axis=-1, keepdims=True)
        yc = y - mu
        yn = yc * lax.rsqrt(jnp.mean(yc * yc, axis=-1, keepdims=True) + EPS)
        gate = g_ref[chunk(n), :].astype(F32)
        ret_ref[chunk(n), :] = (_silu(gate) * yn).astype(ret_ref.dtype)
        return carry

    lax.fori_loop(0, n_chunks, out_step, 0, unroll=unroll)

    zpad[pl.ds(0, 8), :] = jnp.zeros((8, LANES), F32)
    zpad[pl.ds(8 + seq_len, 8), :] = jnp.zeros((8, LANES), F32)
    for r0 in range(0, seq_len, blk):
        rows = pl.ds(r0, blk)
        zpad[pl.ds(8 + r0, blk), :] = c_ref[rows, :].astype(F32) * u_ref[rows, :].astype(F32)
    cw = cw_ref[...]
    for r0 in range(0, seq_len, blk):
        rows = pl.ds(r0, blk)
        conv = (cw[0:1, :] * zpad[pl.ds(7 + r0, blk), :] + cw[1:2, :] * zpad[pl.ds(8 + r0, blk), :]
                + cw[2:3, :] * zpad[pl.ds(9 + r0, blk), :])
        y = b_ref[rows, :].astype(F32) * conv
        conv_ref[rows, :] = (y * lax.rsqrt(jnp.mean(y * y, axis=-1, keepdims=True) + EPS)
                             ).astype(conv_ref.dtype)


def _mixer(proj, dec, conv_w, rope_tabs, states, *, batch, seq_len, row0):
    d_half = RET_HEADS * HEAD_DIM
    rb0 = row0 // seq_len
    rope = rope_tabs is not None
    state_in = states is not None
    n_chunks = seq_len // CHUNK

    def col(cb):
        return pl.BlockSpec((seq_len, LANES), lambda b, j: (rb0 + b, cb * RET_HEADS + j))

    in_specs = [col(i) for i in range(7)]
    args = [proj] * 7
    in_specs += [pl.BlockSpec((2, None, 8, LANES), lambda b, j: (0, j, 0, 0)),
                 pl.BlockSpec((CONV_WIDTH, LANES), lambda b, j: (0, j))]
    args += [dec, conv_w]
    if rope:
        in_specs += [pl.BlockSpec((seq_len, LANES), lambda b, j: (0, 0))] * 2
        args += list(rope_tabs)
    state_spec = pl.BlockSpec((None, None, 2, HEAD_DIM, HEAD_DIM), lambda b, j: (b, j, 0, 0, 0))
    if state_in:
        in_specs.append(state_spec)
        args.append(states)
    out_spec = pl.BlockSpec((seq_len, LANES), lambda b, j: (b, j))
    out_specs = [out_spec, out_spec]
    out_shape = [jax.ShapeDtypeStruct((batch * seq_len, d_half), BF16)] * 2
    if not state_in:
        out_specs.append(state_spec)
        out_shape.append(jax.ShapeDtypeStruct((batch, RET_HEADS, 2, HEAD_DIM, HEAD_DIM), F32))
    return pl.pallas_call(
        functools.partial(_mixer_kernel, seq_len=seq_len, rope=rope, state_in=state_in,
                          state_out=not state_in),
        grid=(batch, RET_HEADS),
        in_specs=in_specs,
        out_specs=out_specs,
        out_shape=out_shape,
        scratch_shapes=[pltpu.VMEM((seq_len, LANES), BF16), pltpu.VMEM((seq_len, LANES), BF16),
                        pltpu.VMEM((n_chunks, 2 * HEAD_DIM, HEAD_DIM), F32),
                        pltpu.VMEM((n_chunks, 2 * HEAD_DIM, HEAD_DIM), BF16),
                        pltpu.VMEM((seq_len + 16, LANES), F32)],
        compiler_params=_params(("arbitrary", "arbitrary")),
        name="mixer_latent" if state_in else "mixer_context",
    )(*args)


def _residual_next(y, x, gain_post, gate, nxt):
    x_new = x + gate * _rms(y, gain_post)
    if nxt is None:
        return x_new, None
    gain_n, shift_n, scale_n = nxt
    return x_new, _rms(x_new, gain_n) * (1.0 + scale_n) + shift_n


def _route_top2(logits):
    lane = lax.broadcasted_iota(jnp.int32, logits.shape, 1)
    neg = jnp.float32(-jnp.inf)
    l1 = jnp.where(lane < N_EXPERTS, logits, neg)
    m1 = jnp.max(l1, axis=-1, keepdims=True)
    i1 = jnp.min(jnp.where(l1 == m1, lane, LANES), axis=-1, keepdims=True)
    l2 = jnp.where(lane == i1, neg, l1)
    m2 = jnp.max(l2, axis=-1, keepdims=True)
    i2 = jnp.min(jnp.where(l2 == m2, lane, LANES), axis=-1, keepdims=True)
    e2 = jnp.exp(m2 - m1)
    p1 = 1.0 / (1.0 + e2)
    p2 = e2 / (1.0 + e2)
    return jnp.where(lane == 0, i1.astype(F32),
                     jnp.where(lane == 1, i2.astype(F32),
                               jnp.where(lane == 2, p1, jnp.where(lane == 3, p2, 0.0))))


def _out_proj_kernel(*refs, n_lat_tiles, moe):
    (lr_ref, lc_ref, cr_ref, cc_ref, w_ref, x_ref, gp_ref, gt_ref, gn_ref, sh_ref, sc_ref) = refs[:11]
    if moe:
        rt_ref, xo_ref, hp_ref, ri_ref = refs[11:]
    else:
        xo_ref, h_ref = refs[11:]
    half = lr_ref.shape[1]

    def finish(mr_ref, mc_ref):
        y = _dot(mr_ref[...], w_ref[pl.ds(0, half), :]) + _dot(mc_ref[...], w_ref[pl.ds(half, half), :])
        x_new, h = _residual_next(y, x_ref[...], gp_ref[...], gt_ref[...],
                                  (gn_ref[...], sh_ref[...], sc_ref[...]))
        xo_ref[...] = x_new
        if moe:
            r = rt_ref[...]
            h_hi = h.astype(BF16)
            h_lo = (h - h_hi.astype(F32)).astype(BF16)
            r_hi = r.astype(BF16)
            r_lo = (r - r_hi.astype(F32)).astype(BF16)
            ri_ref[...] = _route_top2(_dot(h_hi, r_hi) + (_dot(h_lo, r_hi) + _dot(h_hi, r_lo)))
            hp_ref[...] = _pack_halves(h)
        else:
            h_ref[...] = h.astype(h_ref.dtype)

    @pl.when(pl.program_id(0) < n_lat_tiles)
    def _():
        finish(lr_ref, lc_ref)

    @pl.when(pl.program_id(0) >= n_lat_tiles)
    def _():
        finish(cr_ref, cc_ref)


def _out_proj(mix_lat, mix_ctx, w_out, router_pad, xs, mod, gains, layer, seq, batch, n_rows, tm):
    d = xs.shape[1]
    half = mix_lat[0].shape[1]
    moe = router_pad is not None
    rows = _Rows(tm, d, seq, batch, layer)
    n_lat_tiles = mix_lat[0].shape[0] // tm
    n_ctx_tiles = mix_ctx[0].shape[0] // tm
    lat_spec = pl.BlockSpec((tm, half), lambda t: (jnp.minimum(t, n_lat_tiles - 1), 0))
    ctx_spec = pl.BlockSpec((tm, half), lambda t: (jnp.clip(t - n_lat_tiles, 0, n_ctx_tiles - 1), 0))
    in_specs = [lat_spec, lat_spec, ctx_spec, ctx_spec, pl.BlockSpec((d, d), lambda t: (0, 0)),
                rows.tile(), rows.gain(1), rows.mod(2), rows.gain(2), rows.mod(3), rows.mod(4)]
    args = [mix_lat[0], mix_lat[1], mix_ctx[0], mix_ctx[1], w_out, xs, gains, mod, gains, mod, mod]
    out_specs = [rows.tile()]
    out_shape = [jax.ShapeDtypeStruct((n_rows, d), F32)]
    if moe:
        in_specs.append(pl.BlockSpec((d, LANES), lambda t: (0, 0)))
        args.append(router_pad)
        out_specs += [rows.tile(d // 2), rows.tile(LANES)]
        out_shape += [jax.ShapeDtypeStruct((n_rows, d // 2), U32),
                      jax.ShapeDtypeStruct((n_rows, LANES), F32)]
    else:
        out_specs.append(rows.tile())
        out_shape.append(jax.ShapeDtypeStruct((n_rows, d), BF16))
    return pl.pallas_call(
        functools.partial(_out_proj_kernel, n_lat_tiles=n_lat_tiles, moe=moe),
        grid=(n_rows // tm,),
        in_specs=in_specs,
        out_specs=out_specs,
        out_shape=out_shape,
        compiler_params=_params(("arbitrary",)),
        name="out_proj",
    )(*args)


def _swiglu_step(h, w1_ref, w3_ref, w2_ref, y_ref):
    a = (_silu(_dot(h, w1_ref[...].astype(BF16))) * _dot(h, w3_ref[...].astype(BF16))).astype(BF16)
    y_ref[...] += _dot(a, w2_ref[...].astype(BF16))


def _ffn_kernel(h_ref, w1_ref, w3_ref, w2_ref, y_ref):
    @pl.when(pl.program_id(1) == 0)
    def _():
        y_ref[...] = jnp.zeros(y_ref.shape, F32)

    _swiglu_step(h_ref[...], w1_ref, w3_ref, w2_ref, y_ref)


def _dense_ffn(h, w1, w3, w2, idx, tm):
    t_rows, d = h.shape
    ff = w1.shape[2]
    tf = _pick_tile(256, ff)
    return pl.pallas_call(
        _ffn_kernel,
        grid=(t_rows // tm, ff // tf),
        in_specs=[pl.BlockSpec((tm, d), lambda t, f: (t, 0)),
                  pl.BlockSpec((None, d, tf), lambda t, f: (idx, 0, f)),
                  pl.BlockSpec((None, d, tf), lambda t, f: (idx, 0, f)),
                  pl.BlockSpec((None, tf, d), lambda t, f: (idx, f, 0))],
        out_specs=pl.BlockSpec((tm, d), lambda t, f: (t, 0)),
        out_shape=jax.ShapeDtypeStruct((t_rows, d), F32),
        compiler_params=_params(("arbitrary", "arbitrary")),
        name="dense_ffn",
    )(h, w1, w3, w2)


def _epilogue_kernel(*refs, emit_next):
    if emit_next:
        y_ref, x_ref, gp_ref, gt_ref, gn_ref, sh_ref, sc_ref, xo_ref, hn_ref = refs
        nxt = (gn_ref[...], sh_ref[...], sc_ref[...])
    else:
        y_ref, x_ref, gp_ref, gt_ref, xo_ref = refs
        nxt = None
    x_new, hn = _residual_next(y_ref[...], x_ref[...], gp_ref[...], gt_ref[...], nxt)
    xo_ref[...] = x_new
    if emit_next:
        hn_ref[...] = hn.astype(hn_ref.dtype)


def _ffn_epilogue(y, xs, mod, gains, layer, seq, batch, tm):
    t_rows, d = xs.shape
    rows = _Rows(tm, d, seq, batch, layer)
    return pl.pallas_call(
        functools.partial(_epilogue_kernel, emit_next=True),
        grid=(t_rows // tm,),
        in_specs=[rows.tile(), rows.tile(), rows.gain(3), rows.mod(5)] + rows.next_layer(),
        out_specs=[rows.tile(), rows.tile()],
        out_shape=[jax.ShapeDtypeStruct((t_rows, d), F32), jax.ShapeDtypeStruct((t_rows, d), BF16)],
        compiler_params=_params(("arbitrary",)),
        name="ffn_epilogue",
    )(y, xs, gains, mod, gains, mod, mod)


def _dispatch_plan(rinfo, tm):
    t_rows = rinfo.shape[0]
    n_flat = 2 * t_rows
    n_tiles = n_flat // tm + N_EXPERTS
    e_flat = rinfo[:, 0:2].astype(jnp.int32).T.reshape(-1)
    onehot = (e_flat[:, None] == jnp.arange(N_EXPERTS, dtype=jnp.int32)[None, :]).astype(jnp.int32)
    csum = jnp.cumsum(onehot, axis=0)
    counts = csum[-1]
    padded = (counts + tm - 1) // tm * tm
    ends_p = jnp.cumsum(padded)
    starts_p = ends_p - padded
    pos = jnp.sum(onehot * (csum - 1 + starts_p[None, :]), axis=1).astype(jnp.int32)
    tile_start = jnp.arange(n_tiles, dtype=jnp.int32) * tm
    tile_e = jnp.minimum(jnp.sum(tile_start[:, None] >= ends_p[None, :], axis=1), N_EXPERTS - 1)
    tile_nv = jnp.where(tile_start < ends_p[-1],
                        jnp.clip(counts[tile_e] - (tile_start - starts_p[tile_e]), 0, tm), 0)
    src_tok = jnp.zeros(((n_tiles + 1) * tm,), jnp.int32).at[pos].set(
        jnp.arange(n_flat, dtype=jnp.int32) % t_rows, unique_indices=True)
    last_used = jnp.maximum(jnp.sum(tile_nv > 0) - 1, 0)
    tile_e = jnp.where(tile_nv > 0, tile_e, tile_e[last_used])
    return tile_e.astype(jnp.int32), tile_nv.astype(jnp.int32), src_tok, pos


def _moe_kernel(te_ref, nv_ref, src_ref, h_hbm, w1_ref, w3_ref, w2_ref, y_ref, hbuf, hb, sem,
                *, tm, rows_per_step):
    t = pl.program_id(0)
    f = pl.program_id(1)
    slot = t % 2
    half = hbuf.shape[2]

    def gather_row(row, dst_slot, r):
        pltpu.make_async_copy(h_hbm.at[pl.ds(src_ref[row], 1), :],
                              hbuf.at[dst_slot, pl.ds(r, 1), :], sem.at[dst_slot]).start()

    valid = nv_ref[t] > 0
    prev_valid = nv_ref[jnp.maximum(t - 1, 0)] > 0

    @pl.when((f == 0) & (t == 0))
    def _():
        def issue(r, carry):
            gather_row(r, 0, r)
            return carry

        lax.fori_loop(0, tm, issue, 0)

    @pl.when((f == 0) & ((t == 0) | prev_valid))
    def _():
        pltpu.make_async_copy(h_hbm.at[pl.ds(0, tm), :], hbuf.at[slot], sem.at[slot]).wait()

    @pl.when(f == 0)
    def _():
        y_ref[...] = jnp.zeros(y_ref.shape, F32)

    @pl.when(valid)
    def _():
        @pl.when(f == 0)
        def _():
            lo, hi = _unpack_halves(hbuf[slot])
            hb[:, pl.ds(0, half)] = lo
            hb[:, pl.ds(half, half)] = hi

        nxt = (t + 1) * tm + f * rows_per_step
        for r in range(rows_per_step):
            gather_row(nxt + r, 1 - slot, f * rows_per_step + r)
        _swiglu_step(hb[...], w1_ref, w3_ref, w2_ref, y_ref)


def _moe_ffn(h_packed, plan, w1, w3, w2, idx, tm):
    tile_e, tile_nv, src_tok, _ = plan
    t_rows, half = h_packed.shape
    d = 2 * half
    ff = w1.shape[3]
    tf = _pick_tile(256, ff)
    n_tiles = tile_e.shape[0]
    n_f = ff // tf
    assert tm % n_f == 0

    def f_eff(t, f, nv):
        return jnp.where(nv[t] > 0, f, n_f - 1)

    grid_spec = pltpu.PrefetchScalarGridSpec(
        num_scalar_prefetch=3,
        grid=(n_tiles, n_f),
        in_specs=[pl.BlockSpec(memory_space=pl.ANY),
                  pl.BlockSpec((None, None, d, tf), lambda t, f, te, nv, s: (idx, te[t], 0, f_eff(t, f, nv))),
                  pl.BlockSpec((None, None, d, tf), lambda t, f, te, nv, s: (idx, te[t], 0, f_eff(t, f, nv))),
                  pl.BlockSpec((None, None, tf, d), lambda t, f, te, nv, s: (idx, te[t], f_eff(t, f, nv), 0))],
        out_specs=pl.BlockSpec((tm, d), lambda t, f, te, nv, s: (t, 0)),
        scratch_shapes=[pltpu.VMEM((2, tm, half), U32), pltpu.VMEM((tm, d), BF16),
                        pltpu.SemaphoreType.DMA((2,))])
    return pl.pallas_call(
        functools.partial(_moe_kernel, tm=tm, rows_per_step=tm // n_f),
        grid_spec=grid_spec,
        out_shape=jax.ShapeDtypeStruct((n_tiles * tm, d), F32),
        compiler_params=_params(("arbitrary", "arbitrary")),
        name="moe_ffn",
    )(tile_e, tile_nv, src_tok, h_packed, w1, w3, w2)


def _combine_kernel(*refs, tm, t_rows, emit_next):
    pos_ref, y_hbm, ri_ref, x_ref, gp_ref, gt_ref = refs[:6]
    if emit_next:
        gn_ref, sh_ref, sc_ref, xo_ref, hn_ref, ybuf, sem = refs[6:]
        nxt = (gn_ref[...], sh_ref[...], sc_ref[...])
    else:
        xo_ref, ybuf, sem = refs[6:]
        nxt = None
    i = pl.program_id(0)
    n = pl.num_programs(0)
    slot = i % 2

    def gather_row(tile, dst_slot, r):
        tok = tile * tm + r % tm
        src = pos_ref[(r // tm) * t_rows + tok]
        pltpu.make_async_copy(y_hbm.at[pl.ds(src, 1), :], ybuf.at[dst_slot, pl.ds(r, 1), :],
                              sem.at[dst_slot]).start()

    def wait(s):
        pltpu.make_async_copy(y_hbm.at[pl.ds(0, 2 * tm), :], ybuf.at[s], sem.at[s]).wait()

    @pl.when(i == 0)
    def _():
        def issue(r, carry):
            gather_row(0, 0, r)
            return carry

        lax.fori_loop(0, 2 * tm, issue, 0)

    wait(slot)
    nxt_tile = jnp.minimum(i + 1, n - 1)
    for r in range(2 * tm):
        gather_row(nxt_tile, 1 - slot, r)
    ri = ri_ref[...]
    y = ri[:, 2:3] * ybuf[slot, pl.ds(0, tm), :] + ri[:, 3:4] * ybuf[slot, pl.ds(tm, tm), :]
    x_new, hn = _residual_next(y, x_ref[...], gp_ref[...], gt_ref[...], nxt)
    xo_ref[...] = x_new
    if emit_next:
        hn_ref[...] = hn.astype(hn_ref.dtype)

    @pl.when(i == n - 1)
    def _():
        wait(1 - slot)


def _moe_combine(y_sorted, pos, rinfo, xs, mod, gains, layer, seq, batch, tm, emit_next):
    t_rows, d = xs.shape
    rows = _Rows(tm, d, seq, batch, layer)
    in_specs = [pl.BlockSpec(memory_space=pl.ANY), rows.tile(LANES), rows.tile(), rows.gain(3), rows.mod(5)]
    args = [y_sorted, rinfo, xs, gains, mod]
    out_specs = [rows.tile()]
    out_shape = [jax.ShapeDtypeStruct((t_rows, d), F32)]
    if emit_next:
        in_specs += rows.next_layer()
        args += [gains, mod, mod]
        out_specs.append(rows.tile())
        out_shape.append(jax.ShapeDtypeStruct((t_rows, d), BF16))
    grid_spec = pltpu.PrefetchScalarGridSpec(
        num_scalar_prefetch=1,
        grid=(t_rows // tm,),
        in_specs=in_specs,
        out_specs=out_specs,
        scratch_shapes=[pltpu.VMEM((2, 2 * tm, d), F32), pltpu.SemaphoreType.DMA((2,))])
    out = pl.pallas_call(
        functools.partial(_combine_kernel, tm=tm, t_rows=t_rows, emit_next=emit_next),
        grid_spec=grid_spec,
        out_shape=out_shape,
        compiler_params=_params(("arbitrary",)),
        name="moe_combine",
    )(pos, *args)
    return (out[0], out[1]) if emit_next else (out[0], None)


def _rope_tables(seq):
    pos = jnp.arange(seq, dtype=jnp.int32)
    row = (pos // GRID_W).astype(F32)
    col = (pos % GRID_W).astype(F32)
    n_freq = HEAD_DIM // 4
    freqs = ROPE_BASE ** (-jnp.arange(n_freq, dtype=F32) / n_freq)
    ang_r = row[:, None] * freqs[None, :]
    ang_c = col[:, None] * freqs[None, :]
    cos = jnp.concatenate([jnp.cos(ang_r), jnp.cos(ang_r), jnp.cos(ang_c), jnp.cos(ang_c)], axis=-1)
    sin = jnp.concatenate([-jnp.sin(ang_r), jnp.sin(ang_r), -jnp.sin(ang_c), jnp.sin(ang_c)], axis=-1)
    return cos, sin


def kernel(x, c, ctx, c_ctx, w_in, w_out, conv_w, ret_decay, ada_w, ada_b, norm_w, ffn_w1, ffn_w3,
           ffn_w2, router, moe_w1, moe_w3, moe_w2):
    batch, seq, d = x.shape
    n_ctx = ctx.shape[1]
    depth = w_in.shape[0]
    n_lat = batch * seq
    assert d == 2 * RET_HEADS * HEAD_DIM and batch + 1 <= MOD_ROWS
    assert seq % CHUNK == 0 and n_ctx % CHUNK == 0 and n_lat % n_ctx == 0

    tm_small = _pick_tile(256, seq, batch * n_ctx)
    tm_big = _pick_tile(1024, seq, batch * n_ctx)

    c8 = jnp.zeros((MOD_ROWS, d), F32).at[:batch].set(c).at[batch].set(c_ctx)
    mod = _ada_table(c8, ada_w, ada_b).reshape(depth * MOD_ROWS, 1, 6 * d)
    gains = norm_w.reshape(depth * 4, 1, d)
    rope_tabs = _rope_tables(seq)
    dec = jnp.broadcast_to(ret_decay.astype(F32)[:, :, :, None, None], (depth, 2, RET_HEADS, 8, LANES))
    router_pad = jnp.pad(router, ((0, 0), (0, 0), (0, LANES - N_EXPERTS)))

    xs = jnp.concatenate([x.reshape(n_lat, d), ctx.reshape(batch * n_ctx, d)], axis=0)
    h = _prologue(xs, mod, gains, seq, batch, tm_small)

    for layer in range(depth):
        last = layer == depth - 1
        is_moe = layer % 2 == 1
        i = layer // 2
        proj = _in_proj(h, w_in, layer, tm_big)
        ctx_r, ctx_c, states = _mixer(proj, dec[layer], conv_w[layer], None, None,
                                      batch=batch, seq_len=n_ctx, row0=n_lat)
        lat_r, lat_c = _mixer(proj, dec[layer], conv_w[layer], rope_tabs, states,
                              batch=batch, seq_len=seq, row0=0)
        n_rows = n_lat if last else xs.shape[0]
        out = _out_proj((lat_r, lat_c), (ctx_r, ctx_c), w_out[layer].astype(BF16),
                        router_pad[i] if is_moe else None, xs, mod, gains, layer, seq, batch,
                        n_rows, tm_small)
        if not is_moe:
            xs, h2 = out
            y = _dense_ffn(h2, ffn_w1, ffn_w3, ffn_w2, i, tm_big)
            xs, h = _ffn_epilogue(y, xs, mod, gains, layer, seq, batch, tm_small)
        else:
            xs, h_packed, rinfo = out
            plan = _dispatch_plan(rinfo, tm_big)
            y_sorted = _moe_ffn(h_packed, plan, moe_w1, moe_w3, moe_w2, i, tm_big)
            xs, h = _moe_combine(y_sorted, plan[3], rinfo, xs, mod, gains, layer, seq, batch,
                                 tm_small, not last)
    return xs.reshape(batch, seq, d)
```
